```python
import math
import jax, jax.numpy as jnp
from jax import lax
import numpy as np

D_MODEL = 2048
BATCH = 2
SEQ = 4096
DEPTH = 4

GRID_W = 64
CTX_LEN = 256
N_MIXERS = 2
N_S5_LAYERS = (DEPTH + 1) // 2
N_MLA_LAYERS = DEPTH // 2

S5_GROUP = 16
S5_GROUPS = D_MODEL // S5_GROUP
S5_STATE = 64
S5_DT_MIN = 1e-3
S5_DT_MAX = 1e-1

MLA_HEADS = 16
Q_LORA = 512
KV_LORA = 512
QK_NOPE = 128
QK_ROPE = 64
V_HEAD = 128
ROPE_THETA = 10000.0
ATTN_BLOCK = 128

D_FF = 4 * D_MODEL
EPS = 1e-6

kernel_name = "hybrid_s5_mla_dit_trunk"

F32 = jnp.float32


def rmsnorm(x, g):
    xf = x.astype(F32)
    y = xf * lax.rsqrt(jnp.mean(xf * xf, axis=-1, keepdims=True) + EPS)
    return (y * g.astype(F32)).astype(x.dtype)


def sq_relu_mlp(h, w1, w2):
    return jnp.square(jax.nn.relu(h @ w1)) @ w2


def axial_rope_tables(n_tok):
    rows = n_tok // GRID_W
    row = jnp.repeat(jnp.arange(rows, dtype=F32), GRID_W)
    col = jnp.tile(jnp.arange(GRID_W, dtype=F32), rows)
    n_freq = QK_ROPE // 4
    inv = ROPE_THETA ** (-jnp.arange(n_freq, dtype=F32) / n_freq)
    ang = jnp.stack([row[:, None] * inv, col[:, None] * inv], axis=1)
    return jnp.cos(ang), jnp.sin(ang)


def apply_axial_rope(x, cos, sin):
    xs = x.reshape(x.shape[:-1] + (2, 2, QK_ROPE // 4)).astype(F32)
    x1 = xs[..., 0, :]
    x2 = xs[..., 1, :]
    out = jnp.stack([x1 * cos - x2 * sin, x1 * sin + x2 * cos], axis=-2)
    return out.reshape(x.shape).astype(x.dtype)


def mla_queries(h, w_dq, q_norm, w_uq):
    B, L, _ = h.shape
    q = (rmsnorm(h @ w_dq, q_norm) @ w_uq).reshape(B, L, MLA_HEADS, QK_NOPE + QK_ROPE)
    return q[..., :QK_NOPE], q[..., QK_NOPE:]


def mla_keys_values(h, w_dkv, kv_norm, w_ukv):
    B, L, _ = h.shape
    ckv = h @ w_dkv
    kv = (rmsnorm(ckv[..., :KV_LORA], kv_norm) @ w_ukv).reshape(B, L, MLA_HEADS, QK_NOPE + V_HEAD)
    return kv[..., :QK_NOPE], ckv[..., KV_LORA:], kv[..., QK_NOPE:]


def mla_attend(q_nope, q_pe, k_nope, k_pe, v):
    scale = (QK_NOPE + QK_ROPE) ** -0.5
    s = (jnp.einsum('bqhd,bkhd->bhqk', q_nope, k_nope)
         + jnp.einsum('bqhr,bkr->bhqk', q_pe, k_pe)).astype(F32) * scale
    p = jax.nn.softmax(s, axis=-1).astype(v.dtype)
    return jnp.einsum('bhqk,bkhd->bqhd', p, v)


def mla_attend_blocked(q_nope, q_pe, k_nope, k_pe, v):
    B, L = q_nope.shape[:2]
    nb = L // ATTN_BLOCK

    def to_blocks(t):
        return jnp.moveaxis(t.reshape((B, nb, ATTN_BLOCK) + t.shape[2:]), 1, 0)

    out = lax.map(lambda qs: mla_attend(qs[0], qs[1], k_nope, k_pe, v),
                  (to_blocks(q_nope), to_blocks(q_pe)))
    return jnp.moveaxis(out, 0, 1).reshape(B, L, MLA_HEADS, V_HEAD)


def mla_mixer(h_lat, h_ctx, cos, sin, w_dq, q_norm, w_uq, w_dkv, kv_norm, w_ukv, w_o, need_ctx):
    B, L, _ = h_lat.shape
    qn, qp = mla_queries(h_lat, w_dq, q_norm, w_uq)
    qp = apply_axial_rope(qp, cos[:, None], sin[:, None])
    kn, kp, v = mla_keys_values(h_lat, w_dkv, kv_norm, w_ukv)
    kp = apply_axial_rope(kp, cos, sin)
    ckn, ckp, cv = mla_keys_values(h_ctx, w_dkv, kv_norm, w_ukv)
    kn_all = jnp.concatenate([ckn, kn], axis=1)
    kp_all = jnp.concatenate([ckp, kp], axis=1)
    v_all = jnp.concatenate([cv, v], axis=1)
    o_lat = mla_attend_blocked(qn, qp, kn_all, kp_all, v_all).reshape(B, L, MLA_HEADS * V_HEAD) @ w_o
    if not need_ctx:
        return o_lat, None
    cqn, cqp = mla_queries(h_ctx, w_dq, q_norm, w_uq)
    o_ctx = mla_attend(cqn, cqp, ckn, ckp, cv).reshape(B, h_ctx.shape[1], MLA_HEADS * V_HEAD) @ w_o
    return o_lat, o_ctx


def s5_discretize(lam_re, lam_im, log_dt, b_re, b_im):
    lam_re = lam_re.astype(F32)
    lam_im = lam_im.astype(F32)
    dt = jnp.exp(log_dt.astype(F32))[:, None]
    mag = jnp.exp(lam_re * dt)
    lbr = mag * jnp.cos(lam_im * dt)
    lbi = mag * jnp.sin(lam_im * dt)
    nr = lbr - 1.0
    den = lam_re * lam_re + lam_im * lam_im
    fr = (nr * lam_re + lbi * lam_im) / den
    fi = (lbi * lam_re - nr * lam_im) / den
    b_re = b_re.astype(F32)
    b_im = b_im.astype(F32)
    bbr = fr[..., None] * b_re - fi[..., None] * b_im
    bbi = fr[..., None] * b_im + fi[..., None] * b_re
    return lbr, lbi, bbr, bbi


def _linrec_combine(e1, e2):
    a1r, a1i, b1r, b1i = e1
    a2r, a2i, b2r, b2i = e2
    return (a2r * a1r - a2i * a1i,
            a2r * a1i + a2i * a1r,
            a2r * b1r - a2i * b1i + b2r,
            a2r * b1i + a2i * b1r + b2i)


def s5_scan(u, lbr, lbi, bbr, bbi, h0r, h0i, reverse):
    bur = jnp.einsum('blgc,gpc->blgp', u, bbr)
    bui = jnp.einsum('blgc,gpc->blgp', u, bbi)
    if h0r is not None:
        first = -1 if reverse else 0
        bur = bur.at[:, first].add(lbr * h0r - lbi * h0i)
        bui = bui.at[:, first].add(lbr * h0i + lbi * h0r)
    L = u.shape[1]
    ar = jnp.broadcast_to(lbr, (1, L) + lbr.shape)
    ai = jnp.broadcast_to(lbi, (1, L) + lbi.shape)
    _, _, hr, hi = lax.associative_scan(_linrec_combine, (ar, ai, bur, bui), axis=1, reverse=reverse)
    return hr, hi


def s5_readout(hr, hi, c_re, c_im):
    return (jnp.einsum('blgp,gcp->blgc', hr, c_re.astype(F32))
            - jnp.einsum('blgp,gcp->blgc', hi, c_im.astype(F32)))


def s5_glu(y, w_glu, b_glu):
    z = jax.nn.gelu(y)
    return z * jax.nn.sigmoid(z @ w_glu.astype(F32) + b_glu.astype(F32))


def s5_mixer(h_lat, h_ctx, lam_re, lam_im, log_dt, b_re, b_im, c_re, c_im, d_skip, w_glu, b_glu, need_ctx):
    B, L, D = h_lat.shape
    Lc = h_ctx.shape[1]
    u_lat = h_lat.astype(F32).reshape(B, L, S5_GROUPS, S5_GROUP)
    u_ctx = h_ctx.astype(F32).reshape(B, Lc, S5_GROUPS, S5_GROUP)
    d_g = d_skip.astype(F32).reshape(S5_GROUPS, S5_GROUP)
    y_lat = d_g * u_lat
    y_ctx = d_g * u_ctx if need_ctx else None
    for direction in range(2):
        rev = direction == 1
        disc = s5_discretize(lam_re[direction], lam_im[direction], log_dt[direction],
                             b_re[direction], b_im[direction])
        cr, ci = s5_scan(u_ctx, *disc, None, None, rev)
        fin = 0 if rev else -1
        hr, hi = s5_scan(u_lat, *disc, cr[:, fin], ci[:, fin], rev)
        y_lat = y_lat + s5_readout(hr, hi, c_re[direction], c_im[direction])
        if need_ctx:
            y_ctx = y_ctx + s5_readout(cr, ci, c_re[direction], c_im[direction])
    o_lat = s5_glu(y_lat.reshape(B, L, D), w_glu, b_glu).astype(h_lat.dtype)
    if not need_ctx:
        return o_lat, None
    o_ctx = s5_glu(y_ctx.reshape(B, Lc, D), w_glu, b_glu).astype(h_ctx.dtype)
    return o_lat, o_ctx


def setup_inputs(seed: int = 0) -> dict:
    key = jax.random.key(seed)
    ks = jax.random.split(key, 40)

    def nrm(k, shape, scale):
        return jax.random.normal(k, shape, F32) * scale

    G, P = S5_GROUPS, S5_STATE
    n_idx = jnp.arange(P, dtype=F32)
    inp = {}
    inp["x"] = nrm(ks[0], (BATCH, SEQ, D_MODEL), 1.0)
    inp["c"] = nrm(ks[1], (BATCH, D_MODEL), 1.0)
    inp["ctx"] = nrm(ks[2], (BATCH, CTX_LEN, D_MODEL), 1.0)
    inp["c_ctx"] = nrm(ks[3], (D_MODEL,), 1.0)
    inp["ada_w"] = nrm(ks[4], (DEPTH, D_MODEL, 6 * D_MODEL), 0.5 * D_MODEL ** -0.5)
    inp["ada_b"] = nrm(ks[5], (DEPTH, 6 * D_MODEL), 0.02)
    inp["norm1_g"] = 1.0 + nrm(ks[6], (DEPTH, D_MODEL), 0.05)
    inp["norm2_g"] = 1.0 + nrm(ks[7], (DEPTH, D_MODEL), 0.05)
    inp["final_norm_g"] = 1.0 + nrm(ks[8], (D_MODEL,), 0.05)
    inp["s5_lam_re"] = -0.5 + nrm(ks[9], (N_S5_LAYERS, 2, G, P), 0.01)
    inp["s5_lam_im"] = math.pi * n_idx + nrm(ks[10], (N_S5_LAYERS, 2, G, P), 0.01)
    inp["s5_log_dt"] = jax.random.uniform(ks[11], (N_S5_LAYERS, 2, G), F32,
                                          math.log(S5_DT_MIN), math.log(S5_DT_MAX))
    inp["s5_b_re"] = nrm(ks[12], (N_S5_LAYERS, 2, G, P, S5_GROUP), (2 * S5_GROUP) ** -0.5)
    inp["s5_b_im"] = nrm(ks[13], (N_S5_LAYERS, 2, G, P, S5_GROUP), (2 * S5_GROUP) ** -0.5)
    inp["s5_c_re"] = nrm(ks[14], (N_S5_LAYERS, 2, G, S5_GROUP, P), (2 * P) ** -0.5)
    inp["s5_c_im"] = nrm(ks[15], (N_S5_LAYERS, 2, G, S5_GROUP, P), (2 * P) ** -0.5)
    inp["s5_d"] = nrm(ks[16], (N_S5_LAYERS, D_MODEL), 0.5)
    inp["s5_w_glu"] = nrm(ks[17], (N_S5_LAYERS, D_MODEL, D_MODEL), D_MODEL ** -0.5)
    inp["s5_b_glu"] = nrm(ks[18], (N_S5_LAYERS, D_MODEL), 0.02)
    inp["mla_w_dq"] = nrm(ks[19], (N_MLA_LAYERS, D_MODEL, Q_LORA), D_MODEL ** -0.5)
    inp["mla_q_norm"] = 1.0 + nrm(ks[20], (N_MLA_LAYERS, Q_LORA), 0.05)
    inp["mla_w_uq"] = nrm(ks[21], (N_MLA_LAYERS, Q_LORA, MLA_HEADS * (QK_NOPE + QK_ROPE)), Q_LORA ** -0.5)
    inp["mla_w_dkv"] = nrm(ks[22], (N_MLA_LAYERS, D_MODEL, KV_LORA + QK_ROPE), D_MODEL ** -0.5)
    inp["mla_kv_norm"] = 1.0 + nrm(ks[23], (N_MLA_LAYERS, KV_LORA), 0.05)
    inp["mla_w_ukv"] = nrm(ks[24], (N_MLA_LAYERS, KV_LORA, MLA_HEADS * (QK_NOPE + V_HEAD)), KV_LORA ** -0.5)
    inp["mla_w_o"] = nrm(ks[25], (N_MLA_LAYERS, MLA_HEADS * V_HEAD, D_MODEL), (MLA_HEADS * V_HEAD) ** -0.5)
    inp["ffn_w1"] = nrm(ks[26], (DEPTH, D_MODEL, D_FF), D_MODEL ** -0.5)
    inp["ffn_w2"] = nrm(ks[27], (DEPTH, D_FF, D_MODEL), D_FF ** -0.5)
    return inp


def reference(x, c, ctx, c_ctx, ada_w, ada_b, norm1_g, norm2_g, final_norm_g,
              s5_lam_re, s5_lam_im, s5_log_dt, s5_b_re, s5_b_im, s5_c_re, s5_c_im,
              s5_d, s5_w_glu, s5_b_glu,
              mla_w_dq, mla_q_norm, mla_w_uq, mla_w_dkv, mla_kv_norm, mla_w_ukv, mla_w_o,
              ffn_w1, ffn_w2):
    B, L, _ = x.shape
    cos, sin = axial_rope_tables(L)
    s_lat = jax.nn.silu(c)
    s_ctx = jax.nn.silu(c_ctx)
    h_ctx = ctx
    for i in range(DEPTH):
        last = i == DEPTH - 1
        j = i // N_MIXERS
        sh1, sc1, g1, sh2, sc2, g2 = jnp.split((s_lat @ ada_w[i] + ada_b[i])[:, None, :], 6, axis=-1)
        csh1, csc1, cg1, csh2, csc2, cg2 = jnp.split(s_ctx @ ada_w[i] + ada_b[i], 6, axis=-1)
        a_lat = rmsnorm(x, norm1_g[i]) * (1.0 + sc1) + sh1
        a_ctx = rmsnorm(h_ctx, norm1_g[i]) * (1.0 + csc1) + csh1
        if i % N_MIXERS == 0:
            m_lat, m_ctx = s5_mixer(a_lat, a_ctx, s5_lam_re[j], s5_lam_im[j], s5_log_dt[j],
                                    s5_b_re[j], s5_b_im[j], s5_c_re[j], s5_c_im[j],
                                    s5_d[j], s5_w_glu[j], s5_b_glu[j], not last)
        else:
            m_lat, m_ctx = mla_mixer(a_lat, a_ctx, cos, sin, mla_w_dq[j], mla_q_norm[j], mla_w_uq[j],
                                     mla_w_dkv[j], mla_kv_norm[j], mla_w_ukv[j], mla_w_o[j], not last)
        x = x + g1 * m_lat
        x = x + g2 * sq_relu_mlp(rmsnorm(x, norm2_g[i]) * (1.0 + sc2) + sh2, ffn_w1[i], ffn_w2[i])
        if not last:
            h_ctx = h_ctx + cg1 * m_ctx
            h_ctx = h_ctx + cg2 * sq_relu_mlp(rmsnorm(h_ctx, norm2_g[i]) * (1.0 + csc2) + csh2,
                                             ffn_w1[i], ffn_w2[i])
    return rmsnorm(x, final_norm_g)
```

```python
import functools
import math

import jax
import jax.numpy as jnp
from jax import lax
from jax.experimental import pallas as pl
from jax.experimental.pallas import tpu as pltpu

F32 = jnp.float32
BF16 = jnp.bfloat16

EPS = 1e-6
GRID_W = 64
ROPE_THETA = 10000.0

LANES = 128
SUBLANES = 8
CHUNK = 16
VMEM_LIMIT = 56 * 1024 * 1024

TM_TOKEN = 512
TM_PROJ = 256
TQ_ATTN = 256
TK_ATTN = 512
TF_FFN = 1024
TN_ADA = 1024


def _params(sem):
    return pltpu.CompilerParams(dimension_semantics=sem, vmem_limit_bytes=VMEM_LIMIT)


def _rms(x):
    return x * lax.rsqrt(jnp.mean(x * x, axis=-1, keepdims=True) + EPS)


def _norm_mod(x, g, sc, sh):
    return (_rms(x) * g) * (1.0 + sc) + sh


def _gelu_tanh(y):
    c = math.sqrt(2.0 / math.pi)
    return 0.5 * y * (1.0 + jnp.tanh(c * (y + 0.044715 * (y * y * y))))


def _ada_kernel(cc_ref, w_ref, b_ref, o_ref):
    cc = cc_ref[...]
    s = cc * jax.nn.sigmoid(cc)
    o_ref[...] = jnp.dot(s, w_ref[...], preferred_element_type=F32,
                         precision=lax.Precision.HIGHEST) + b_ref[...]


def _ada_table(cc, ada_w, ada_b):
    depth, d, n6 = ada_w.shape
    tn = min(TN_ADA, n6)
    return pl.pallas_call(
        _ada_kernel,
        grid=(depth, n6 // tn),
        in_specs=[pl.BlockSpec((SUBLANES, d), lambda l, n: (0, 0)),
                  pl.BlockSpec((None, d, tn), lambda l, n: (l, 0, n)),
                  pl.BlockSpec((None, 1, tn), lambda l, n: (l, 0, n))],
        out_specs=pl.BlockSpec((None, SUBLANES, tn), lambda l, n: (l, 0, n)),
        out_shape=jax.ShapeDtypeStruct((depth, SUBLANES, n6), F32),
        compiler_params=_params(("parallel", "parallel")),
        name="ada_table",
    )(cc, ada_w, ada_b.reshape(depth, 1, n6))


class _Layout:
    def __init__(self, b, l, lc, d):
        self.b, self.l, self.lc, self.d = b, l, lc, d
        self.n_lat = b * l
        self.n = b * (l + lc)

    def mod_row(self, tm):
        n_lat_tiles = self.n_lat // tm
        per_batch = self.l // tm
        ctx_row = self.b

        def row(i):
            return jnp.where(i < n_lat_tiles, i // per_batch, ctx_row)
        return row

    def mod_spec(self, layer, chunk, tm, ngrid=1):
        row = self.mod_row(tm)
        if ngrid == 1:
            return pl.BlockSpec((None, None, 1, self.d), lambda i: (layer, row(i), 0, chunk))
        return pl.BlockSpec((None, None, 1, self.d), lambda i, f: (layer, row(i), 0, chunk))


def _norm_kernel(x_ref, g_ref, sh_ref, sc_ref, o_ref):
    o_ref[...] = _norm_mod(x_ref[...], g_ref[...], sc_ref[...], sh_ref[...]).astype(o_ref.dtype)


def _norm_call(lay, x, mod, g, layer):
    tm, d = TM_TOKEN, lay.d
    return pl.pallas_call(
        _norm_kernel,
        grid=(lay.n // tm,),
        in_specs=[pl.BlockSpec((tm, d), lambda i: (i, 0)),
                  pl.BlockSpec((None, 1, d), lambda i: (layer, 0, 0)),
                  lay.mod_spec(layer, 0, tm), lay.mod_spec(layer, 1, tm)],
        out_specs=pl.BlockSpec((tm, d), lambda i: (i, 0)),
        out_shape=jax.ShapeDtypeStruct((lay.n, d), BF16),
        compiler_params=_params(("parallel",)),
        name="norm_mod",
    )(x, g, mod, mod)


def _ffn_kernel(x_ref, g_ref, sh_ref, sc_ref, gate_ref, w1_ref, w2_ref, o_ref, a_ref, acc_ref):
    f = pl.program_id(1)

    @pl.when(f == 0)
    def _():
        a_ref[...] = _norm_mod(x_ref[...], g_ref[...], sc_ref[...], sh_ref[...]).astype(BF16)
        acc_ref[...] = jnp.zeros_like(acc_ref)

    h = jnp.dot(a_ref[...], w1_ref[...], preferred_element_type=F32)
    h = jnp.maximum(h, 0.0)
    h = (h * h).astype(BF16)
    acc_ref[...] += jnp.dot(h, w2_ref[...], preferred_element_type=F32)

    @pl.when(f == pl.num_programs(1) - 1)
    def _():
        o_ref[...] = x_ref[...] + gate_ref[...] * acc_ref[...]


def _ffn_call(lay, x, mod, g, w1, w2, layer):
    tm, d = TM_TOKEN, lay.d
    dff = w1.shape[1]
    tf = min(TF_FFN, dff)
    return pl.pallas_call(
        _ffn_kernel,
        grid=(lay.n // tm, dff // tf),
        in_specs=[pl.BlockSpec((tm, d), lambda i, f: (i, 0)),
                  pl.BlockSpec((None, 1, d), lambda i, f: (layer, 0, 0)),
                  lay.mod_spec(layer, 3, tm, 2), lay.mod_spec(layer, 4, tm, 2),
                  lay.mod_spec(layer, 5, tm, 2),
                  pl.BlockSpec((d, tf), lambda i, f: (0, f)),
                  pl.BlockSpec((tf, d), lambda i, f: (f, 0))],
        out_specs=pl.BlockSpec((tm, d), lambda i, f: (i, 0)),
        out_shape=jax.ShapeDtypeStruct((lay.n, d), F32),
        scratch_shapes=[pltpu.VMEM((tm, d), BF16), pltpu.VMEM((tm, d), F32)],
        compiler_params=_params(("parallel", "arbitrary")),
        name="ffn",
    )(x, g, mod, mod, mod, w1, w2)


def _cmul(ar, ai, br, bi):
    return ar * br - ai * bi, ar * bi + ai * br


def _tile_scan(sr, si, cr, ci, ap_ref, d, reverse):
    rows = lax.broadcasted_iota(jnp.int32, sr.shape, 0)

    def shift(v, k):
        if reverse:
            return jnp.where(rows < SUBLANES - k, pltpu.roll(v, SUBLANES - k, 0), 0.0)
        return jnp.where(rows >= k, pltpu.roll(v, k, 0), 0.0)

    pr, pi = sr, si
    for k in (1, 2, 4):
        akr = ap_ref[d, 0, k:k + 1, :]
        aki = ap_ref[d, 1, k:k + 1, :]
        tr, ti = _cmul(akr, aki, shift(pr, k), shift(pi, k))
        pr, pi = pr + tr, pi + ti
    base = SUBLANES if reverse else 0
    tab_r = ap_ref[d, 0, base:base + SUBLANES, :]
    tab_i = ap_ref[d, 1, base:base + SUBLANES, :]
    tr, ti = _cmul(tab_r, tab_i, cr, ci)
    hr, hi = shift(pr, 1) + tr, shift(pi, 1) + ti
    a8r = ap_ref[d, 0, 2 * SUBLANES:2 * SUBLANES + 1, :]
    a8i = ap_ref[d, 1, 2 * SUBLANES:2 * SUBLANES + 1, :]
    tr, ti = _cmul(a8r, a8i, cr, ci)
    last = 0 if reverse else SUBLANES - 1
    return hr, hi, pr[last:last + 1, :] + tr, pi[last:last + 1, :] + ti


def _s5_core_kernel(*refs, nb, kl, kc):
    al_refs = refs[:nb]
    ac_refs = refs[nb:2 * nb]
    cst_ref, cint_ref, kc_ref, ap_ref, yl_ref, yc_ref = refs[2 * nb:2 * nb + 6]
    x_s, w_s, wintra_s, bd_s, s_s, y_s = refs[2 * nb + 6:]
    gpl = LANES // CHUNK
    ns = x_s.shape[1] // LANES
    sw = s_s.shape[2]
    half = sw // 2
    lat_rows = nb * kl

    for s in range(ns):
        for b in range(nb):
            x_s[b * kl:(b + 1) * kl, s * LANES:(s + 1) * LANES] = al_refs[b][s]
            x_s[lat_rows + b * kc:lat_rows + (b + 1) * kc, s * LANES:(s + 1) * LANES] = ac_refs[b][s]

    zero_tile = jnp.zeros((CHUNK, LANES), BF16)
    n_lt = half // LANES
    for d in range(2):
        for s in range(ns):
            for g in range(gpl):
                r0 = g * CHUNK
                tiles = [cst_ref[d, part, s, r0:r0 + CHUNK, :] if lt == g // 2 else zero_tile
                         for part in range(2) for lt in range(n_lt)]
                w_s[d, s * LANES + r0:s * LANES + r0 + CHUNK, :] = jnp.concatenate(tiles, axis=-1)
    x = x_s[...]
    for d in range(2):
        s_s[d] = jnp.dot(x, w_s[d], preferred_element_type=F32)

    n_ctx_tiles = kc // SUBLANES
    n_lat_tiles = kl // SUBLANES

    def scan_body(i, carry):
        new = []
        for d in range(2):
            for b in range(nb):
                cr, ci = carry[2 * (d * nb + b)], carry[2 * (d * nb + b) + 1]
                ic = i if d == 0 else n_ctx_tiles - 1 - i
                il = i - n_ctx_tiles if d == 0 else n_lat_tiles - 1 - (i - n_ctx_tiles)
                row = jnp.where(i < n_ctx_tiles, lat_rows + b * kc + ic * SUBLANES,
                                b * kl + il * SUBLANES)
                row = pl.multiple_of(row, SUBLANES)
                sr = s_s[d, pl.ds(row, SUBLANES), 0:half]
                si = s_s[d, pl.ds(row, SUBLANES), half:sw]
                hr, hi, cr, ci = _tile_scan(sr, si, cr, ci, ap_ref, d, d == 1)
                s_s[d, pl.ds(row, SUBLANES), 0:half] = hr
                s_s[d, pl.ds(row, SUBLANES), half:sw] = hi
                new += [cr, ci]
        return tuple(new)

    zero_row = jnp.zeros((1, half), F32)
    lax.fori_loop(0, n_ctx_tiles + n_lat_tiles, scan_body, (zero_row,) * (4 * nb))

    lane_group = lax.broadcasted_iota(jnp.int32, (CHUNK, LANES), 1) // CHUNK
    n_lag = 2 * ns - 1
    for lag in range(n_lag):
        ktile = kc_ref[lag].astype(F32)
        for g in range(gpl):
            bd_s[lag, g * CHUNK:(g + 1) * CHUNK, :] = jnp.where(lane_group == g, ktile, 0.0).astype(BF16)
    for s in range(ns):
        for t in range(ns):
            wintra_s[s * LANES:(s + 1) * LANES, t * LANES:(t + 1) * LANES] = bd_s[t - s + ns - 1]
    y_s[...] = jnp.dot(x, wintra_s[...], preferred_element_type=F32)

    p_rows = cint_ref.shape[3]
    lane_group_p = lax.broadcasted_iota(jnp.int32, (p_rows, LANES), 1) // CHUNK
    hcols = (ns * LANES) // 2
    for d in range(2):
        for part in range(2):
            for t in range(ns):
                ctile = cint_ref[d, part, t].astype(F32)
                c0 = t * LANES
                rbase = (c0 // hcols) * sw
                cc = c0 % hcols
                for g in range(gpl):
                    r0 = rbase + part * half + g * p_rows
                    w_s[d, r0:r0 + p_rows, cc:cc + LANES] = jnp.where(
                        lane_group_p == g, ctile, 0.0).astype(BF16)
    for d in range(2):
        h = s_s[d].astype(BF16)
        for c in range(2):
            y_s[:, c * hcols:(c + 1) * hcols] += jnp.dot(
                h, w_s[d, c * sw:(c + 1) * sw, :], preferred_element_type=F32)

    for t in range(ns):
        for b in range(nb):
            yl_ref[b * ns + t] = y_s[b * kl:(b + 1) * kl, t * LANES:(t + 1) * LANES]
            yc_ref[b * ns + t] = y_s[lat_rows + b * kc:lat_rows + (b + 1) * kc, t * LANES:(t + 1) * LANES]


def _s5_core_call(lay, a, tabs):
    cst, cint, kct, apow = tabs
    nb, d = lay.b, lay.d
    kl, kc = lay.l // CHUNK, lay.lc // CHUNK
    nj = d // LANES
    rows = nb * (kl + kc)
    sw = (LANES // CHUNK) * cst.shape[-1]
    assert 2 * sw == CHUNK * LANES
    a_l = a.reshape(lay.n // kl, kl, d)
    a_c = a.reshape(lay.n // kc, kc, d)
    ctx_blk0 = (nb * kl) // kc
    in_specs = ([pl.BlockSpec((CHUNK, kl, LANES), functools.partial(lambda j, b: (b, 0, j), b=b))
                 for b in range(nb)]
                + [pl.BlockSpec((CHUNK, kc, LANES), functools.partial(lambda j, b: (ctx_blk0 + b, 0, j), b=b))
                   for b in range(nb)]
                + [pl.BlockSpec((2, 2, None) + cst.shape[3:], lambda j: (0, 0, j, 0, 0, 0)),
                   pl.BlockSpec((2, 2, None) + cint.shape[3:], lambda j: (0, 0, j, 0, 0, 0)),
                   pl.BlockSpec((None,) + kct.shape[1:], lambda j: (j, 0, 0, 0)),
                   pl.BlockSpec((2, 2, None) + apow.shape[3:], lambda j: (0, 0, j, 0, 0))])
    y_l, y_c = pl.pallas_call(
        functools.partial(_s5_core_kernel, nb=nb, kl=kl, kc=kc),
        grid=(nj,),
        in_specs=in_specs,
        out_specs=[pl.BlockSpec((nb * CHUNK, kl, LANES), lambda j: (0, 0, j)),
                   pl.BlockSpec((nb * CHUNK, kc, LANES), lambda j: (0, 0, j))],
        out_shape=[jax.ShapeDtypeStruct((nb * CHUNK, kl, d), F32),
                   jax.ShapeDtypeStruct((nb * CHUNK, kc, d), F32)],
        scratch_shapes=[pltpu.VMEM((rows, CHUNK * LANES), BF16),
                        pltpu.VMEM((2, CHUNK * LANES, sw), BF16),
                        pltpu.VMEM((CHUNK * LANES, CHUNK * LANES), BF16),
                        pltpu.VMEM((2 * CHUNK - 1, LANES, LANES), BF16),
                        pltpu.VMEM((2, rows, sw), F32),
                        pltpu.VMEM((rows, CHUNK * LANES), F32)],
        compiler_params=_params(("parallel",)),
        name="s5_core",
    )(*([a_l] * nb + [a_c] * nb + [cst, cint, kct, apow]))
    return y_l.reshape(nb * lay.l, d), y_c.reshape(nb * lay.lc, d)


def _s5_tables(lam_re, lam_im, log_dt, b_re, b_im, c_re, c_im, d_skip):
    ng, p = lam_re.shape[1:]
    grp = b_re.shape[-1]
    gpl = LANES // grp
    nj = ng // gpl
    hi = lax.Precision.HIGHEST
    dt = jnp.exp(log_dt.astype(F32))[..., None]
    lr, li = lam_re.astype(F32), lam_im.astype(F32)

    def lam_pow(m):
        m = jnp.asarray(m, F32)[:, None, None, None]
        mag = jnp.exp(lr * dt * m)
        return mag * jnp.cos(li * dt * m), mag * jnp.sin(li * dt * m)

    mag = jnp.exp(lr * dt)
    lbr, lbi = mag * jnp.cos(li * dt), mag * jnp.sin(li * dt)
    nr = lbr - 1.0
    den = lr * lr + li * li
    fr = (nr * lr + lbi * li) / den
    fi = (lbi * lr - nr * li) / den
    bbr = fr[..., None] * b_re - fi[..., None] * b_im
    bbi = fr[..., None] * b_im + fi[..., None] * b_re

    steps = jnp.arange(CHUNK)
    pr, pi = lam_pow(steps)
    sel = jnp.stack([steps[::-1], steps], axis=1)
    dir_idx = jnp.arange(2)[None, :]
    spr = pr[sel, dir_idx]
    spi = pi[sel, dir_idx]
    st_re = spr[..., None] * bbr - spi[..., None] * bbi
    st_im = spr[..., None] * bbi + spi[..., None] * bbr

    def lay_st(v):
        v = jnp.transpose(v, (1, 0, 2, 4, 3))
        v = v.reshape(2, CHUNK, nj, gpl, grp, p)
        even = (jnp.arange(gpl) % 2 == 0)[None, None, None, :, None, None]
        v = jnp.concatenate([jnp.where(even, v, 0.0), jnp.where(even, 0.0, v)], axis=-1)
        v = jnp.transpose(v, (0, 2, 1, 3, 4, 5))
        return v.reshape(2, nj, CHUNK, gpl * grp, 2 * p)
    cst = jnp.stack([lay_st(st_re), lay_st(st_im)], axis=1).astype(BF16)

    p1r, p1i = lam_pow(jnp.arange(1, CHUNK + 1))
    selo = jnp.stack([steps, steps[::-1]], axis=1)
    opr = p1r[selo, dir_idx]
    opi = p1i[selo, dir_idx]
    cr, ci = c_re.astype(F32), c_im.astype(F32)
    o_re = cr[None] * opr[:, :, :, None, :] - ci[None] * opi[:, :, :, None, :]
    o_im = cr[None] * opi[:, :, :, None, :] + ci[None] * opr[:, :, :, None, :]

    def lay_o(v):
        v = jnp.transpose(v, (1, 0, 2, 3, 4)).reshape(2, CHUNK, nj, gpl, grp, p)
        v = jnp.transpose(v, (0, 2, 1, 5, 3, 4))
        return v.reshape(2, nj, CHUNK, p, gpl * grp)
    cint = jnp.stack([lay_o(o_re), lay_o(-o_im)], axis=1).astype(BF16)

    k_re = cr[None] * pr[:, :, :, None, :] - ci[None] * pi[:, :, :, None, :]
    k_im = cr[None] * pi[:, :, :, None, :] + ci[None] * pr[:, :, :, None, :]
    kk = (jnp.einsum('ldgcp,dgpk->ldgck', k_re, bbr, precision=hi)
          - jnp.einsum('ldgcp,dgpk->ldgck', k_im, bbi, precision=hi))
    skip = d_skip.astype(F32).reshape(ng, grp)
    k0 = kk[0, 0] + kk[0, 1] + skip[:, :, None] * jnp.eye(grp, dtype=F32)
    ktab = jnp.concatenate([kk[1:, 1][::-1], k0[None], kk[1:, 0]], axis=0)
    ktab = ktab.reshape(2 * CHUNK - 1, nj, gpl, grp, grp)
    kct = jnp.transpose(ktab, (1, 0, 4, 2, 3)).reshape(nj, 2 * CHUNK - 1, grp, gpl * grp).astype(BF16)

    ar, ai = lam_pow(CHUNK * jnp.arange(SUBLANES + 1))

    def lay_a(v):
        tab = jnp.concatenate([v[:SUBLANES], v[:SUBLANES][::-1], v[SUBLANES:],
                               jnp.zeros((SUBLANES - 1,) + v.shape[1:], F32)], axis=0)
        tab = jnp.transpose(tab, (1, 0, 2, 3)).reshape(2, 3 * SUBLANES, nj, gpl * p)
        return jnp.transpose(tab, (0, 2, 1, 3))
    apow = jnp.stack([lay_a(ar), lay_a(ai)], axis=1)
    return cst, cint, kct, apow


def _glu_kernel(yl_ref, yc_ref, x_ref, gate_ref, w_ref, b_ref, o_ref, *, n_lat_tiles):
    i = pl.program_id(0)
    y = jnp.where(i < n_lat_tiles, yl_ref[...], yc_ref[...])
    z = _gelu_tanh(y)
    u = jnp.dot(z.astype(BF16), w_ref[...], preferred_element_type=F32) + b_ref[...]
    o_ref[...] = x_ref[...] + gate_ref[...] * (z * jax.nn.sigmoid(u))


def _glu_call(lay, y_l, y_c, x, mod, w, bias, layer):
    tm, d = TM_TOKEN, lay.d
    nlt = lay.n_lat // tm
    return pl.pallas_call(
        functools.partial(_glu_kernel, n_lat_tiles=nlt),
        grid=(lay.n // tm,),
        in_specs=[pl.BlockSpec((tm, d), lambda i: (jnp.minimum(i, nlt - 1), 0)),
                  pl.BlockSpec((tm, d), lambda i: (jnp.maximum(i - nlt, 0), 0)),
                  pl.BlockSpec((tm, d), lambda i: (i, 0)),
                  lay.mod_spec(layer, 2, tm),
                  pl.BlockSpec((d, d), lambda i: (0, 0)),
                  pl.BlockSpec((1, d), lambda i: (0, 0))],
        out_specs=pl.BlockSpec((tm, d), lambda i: (i, 0)),
        out_shape=jax.ShapeDtypeStruct((lay.n, d), F32),
        compiler_params=_params(("parallel",)),
        name="s5_glu",
    )(y_l, y_c, x, mod, w, bias)


def _rope(v, cos, sin):
    return v * cos + pltpu.roll(v, LANES // 2, 1) * sin


def _mla_proj_kernel(x_ref, g_ref, sh_ref, sc_ref, cos_ref, sin_ref, wdq_ref, qn_ref, wuq_ref,
                     wdkv_ref, kvn_ref, wukv_ref, q_ref, k_ref, v_ref, *, heads, scale):
    a = _norm_mod(x_ref[...], g_ref[...], sc_ref[...], sh_ref[...]).astype(BF16)
    cos, sin = cos_ref[...], sin_ref[...]
    hd = heads * LANES
    cq = jnp.dot(a, wdq_ref[...], preferred_element_type=F32)
    cq = (_rms(cq) * qn_ref[...]).astype(BF16)
    q = jnp.dot(cq, wuq_ref[...], preferred_element_type=F32)
    ckv = jnp.dot(a, wdkv_ref[...], preferred_element_type=F32)
    kvl = ckv.shape[1] - LANES
    ckvn = (_rms(ckv[:, :kvl]) * kvn_ref[...]).astype(BF16)
    kpe = _rope(ckv[:, kvl:], cos, sin).astype(BF16)
    kv = jnp.dot(ckvn, wukv_ref[...], preferred_element_type=F32)
    for h in range(heads):
        lo, hi_ = h * LANES, (h + 1) * LANES
        q_ref[h, :, 0:LANES] = (q[:, lo:hi_] * scale).astype(BF16)
        q_ref[h, :, LANES:2 * LANES] = (_rope(q[:, hd + lo:hd + hi_], cos, sin) * scale).astype(BF16)
        k_ref[h, :, 0:LANES] = kv[:, lo:hi_].astype(BF16)
        k_ref[h, :, LANES:2 * LANES] = kpe
        v_ref[h] = kv[:, hd + lo:hd + hi_].astype(BF16)


def _mla_proj_call(lay, x, mod, g, cos_t, sin_t, wts, layer, heads, scale):
    wdq, qn, wuq, wdkv, kvn, wukv = wts
    tm, d = TM_PROJ, lay.d
    full = lambda arr: pl.BlockSpec(arr.shape, lambda i: (0,) * arr.ndim)
    return pl.pallas_call(
        functools.partial(_mla_proj_kernel, heads=heads, scale=scale),
        grid=(lay.n // tm,),
        in_specs=[pl.BlockSpec((tm, d), lambda i: (i, 0)),
                  pl.BlockSpec((None, 1, d), lambda i: (layer, 0, 0)),
                  lay.mod_spec(layer, 0, tm), lay.mod_spec(layer, 1, tm),
                  pl.BlockSpec((tm, LANES), lambda i: (i, 0)),
                  pl.BlockSpec((tm, LANES), lambda i: (i, 0)),
                  full(wdq), full(qn), full(wuq), full(wdkv), full(kvn), full(wukv)],
        out_specs=[pl.BlockSpec((heads, tm, 2 * LANES), lambda i: (0, i, 0)),
                   pl.BlockSpec((heads, tm, 2 * LANES), lambda i: (0, i, 0)),
                   pl.BlockSpec((heads, tm, LANES), lambda i: (0, i, 0))],
        out_shape=[jax.ShapeDtypeStruct((heads, lay.n, 2 * LANES), BF16),
                   jax.ShapeDtypeStruct((heads, lay.n, 2 * LANES), BF16),
                   jax.ShapeDtypeStruct((heads, lay.n, LANES), BF16)],
        compiler_params=_params(("parallel",)),
        name="mla_proj",
    )(x, g, mod, mod, cos_t, sin_t, wdq, qn, wuq, wdkv, kvn, wukv)


def _attn_kernel(q_ref, kc_ref, vc_ref, kl_ref, vl_ref, o_ref, *, n_lat_q, tk, n_kv):
    qi = pl.program_id(2)
    q = q_ref[...]
    nt = (((1,), (1,)), ((), ()))
    s = lax.dot_general(q, kc_ref[...], nt, preferred_element_type=F32)
    m = jnp.max(s, axis=-1, keepdims=True)
    p = jnp.exp(s - m)
    l = jnp.sum(p, axis=-1, keepdims=True)
    acc = jnp.dot(p.astype(BF16), vc_ref[...], preferred_element_type=F32)

    def body(i, carry):
        m, l, acc = carry
        off = pl.multiple_of(i * tk, tk)
        s = lax.dot_general(q, kl_ref[pl.ds(off, tk), :], nt, preferred_element_type=F32)
        m_new = jnp.maximum(m, jnp.max(s, axis=-1, keepdims=True))
        alpha = jnp.exp(m - m_new)
        p = jnp.exp(s - m_new)
        l = alpha * l + jnp.sum(p, axis=-1, keepdims=True)
        acc = alpha * acc + jnp.dot(p.astype(BF16), vl_ref[pl.ds(off, tk), :], preferred_element_type=F32)
        return m_new, l, acc

    n = jnp.where(qi < n_lat_q, n_kv, 0)
    m, l, acc = lax.fori_loop(0, n, body, (m, l, acc))
    o_ref[...] = (acc / l).astype(o_ref.dtype)


def _attn_call(lay, q, k, v, heads):
    nb, l, lc = lay.b, lay.l, lay.lc
    tq = min(TQ_ATTN, lc)
    tk = min(TK_ATTN, l)
    nlq = l // tq
    ncq = lc // tq
    ctx_blk0 = (nb * l) // lc

    def qrow(b, qi):
        return jnp.where(qi < nlq, b * nlq + qi, nb * nlq + b * ncq + (qi - nlq))
    return pl.pallas_call(
        functools.partial(_attn_kernel, n_lat_q=nlq, tk=tk, n_kv=l // tk),
        grid=(nb, heads, nlq + ncq),
        in_specs=[pl.BlockSpec((None, tq, 2 * LANES), lambda b, h, qi: (h, qrow(b, qi), 0)),
                  pl.BlockSpec((None, lc, 2 * LANES), lambda b, h, qi: (h, ctx_blk0 + b, 0)),
                  pl.BlockSpec((None, lc, LANES), lambda b, h, qi: (h, ctx_blk0 + b, 0)),
                  pl.BlockSpec((None, l, 2 * LANES), lambda b, h, qi: (h, b, 0)),
                  pl.BlockSpec((None, l, LANES), lambda b, h, qi: (h, b, 0))],
        out_specs=pl.BlockSpec((tq, LANES), lambda b, h, qi: (qrow(b, qi), h)),
        out_shape=jax.ShapeDtypeStruct((lay.n, heads * LANES), BF16),
        compiler_params=_params(("parallel", "parallel", "arbitrary")),
        name="mla_attn",
    )(q, k, v, k, v)


def _oproj_kernel(o_ref, x_ref, gate_ref, w_ref, out_ref):
    out_ref[...] = x_ref[...] + gate_ref[...] * jnp.dot(o_ref[...], w_ref[...], preferred_element_type=F32)


def _oproj_call(lay, o, x, mod, w, layer):
    tm, d = TM_TOKEN, lay.d
    return pl.pallas_call(
        _oproj_kernel,
        grid=(lay.n // tm,),
        in_specs=[pl.BlockSpec((tm, o.shape[1]), lambda i: (i, 0)),
                  pl.BlockSpec((tm, d), lambda i: (i, 0)),
                  lay.mod_spec(layer, 2, tm),
                  pl.BlockSpec(w.shape, lambda i: (0, 0))],
        out_specs=pl.BlockSpec((tm, d), lambda i: (i, 0)),
        out_shape=jax.ShapeDtypeStruct((lay.n, d), F32),
        compiler_params=_params(("parallel",)),
        name="mla_oproj",
    )(o, x, mod, w)


def _final_kernel(x_ref, g_ref, o_ref):
    o_ref[...] = _rms(x_ref[...]) * g_ref[...]


def _final_call(lay, x, g):
    tm, d = TM_TOKEN, lay.d
    return pl.pallas_call(
        _final_kernel,
        grid=(lay.n_lat // tm,),
        in_specs=[pl.BlockSpec((tm, d), lambda i: (i, 0)),
                  pl.BlockSpec((1, d), lambda i: (0, 0))],
        out_specs=pl.BlockSpec((tm, d), lambda i: (i, 0)),
        out_shape=jax.ShapeDtypeStruct((lay.n_lat, d), F32),
        compiler_params=_params(("parallel",)),
        name="final_norm",
    )(x, g)


def _to_chunk_major(t):
    b, n, d = t.shape
    return jnp.transpose(t.reshape(b, n // CHUNK, CHUNK, d), (0, 2, 1, 3)).reshape(b * n, d)


def _from_chunk_major(t, b):
    n, d = t.shape[0] // b, t.shape[1]
    return jnp.transpose(t.reshape(b, CHUNK, n // CHUNK, d), (0, 2, 1, 3)).reshape(b, n, d)


def _pad_rope_cols(w):
    nf = w.shape[-1] // 4
    w4 = w.reshape(w.shape[:-1] + (2, 2, nf))
    z = jnp.zeros(w.shape[:-1] + (LANES // 2 - 2 * nf,), w.dtype)
    x1 = w4[..., 0, :].reshape(w.shape[:-1] + (2 * nf,))
    x2 = w4[..., 1, :].reshape(w.shape[:-1] + (2 * nf,))
    return jnp.concatenate([x1, z, x2, z], axis=-1)


def _rope_tables(lay, rope_dim):
    nf = rope_dim // 4
    pos = jnp.arange(lay.l)
    inv = ROPE_THETA ** (-jnp.arange(nf, dtype=F32) / nf)
    row = (pos // GRID_W).astype(F32)
    col = (pos % GRID_W).astype(F32)
    ang = jnp.concatenate([row[:, None] * inv, col[:, None] * inv], axis=1)
    z = jnp.zeros((lay.l, LANES // 2 - 2 * nf), F32)
    cos_l = jnp.concatenate([jnp.cos(ang), z, jnp.cos(ang), z], axis=1)
    sin_l = jnp.concatenate([-jnp.sin(ang), z, jnp.sin(ang), z], axis=1)
    cos_l = _to_chunk_major(jnp.broadcast_to(cos_l[None], (lay.b, lay.l, LANES)))
    sin_l = _to_chunk_major(jnp.broadcast_to(sin_l[None], (lay.b, lay.l, LANES)))
    n_ctx = lay.b * lay.lc
    cos_t = jnp.concatenate([cos_l, jnp.ones((n_ctx, LANES), F32)], axis=0)
    sin_t = jnp.concatenate([sin_l, jnp.zeros((n_ctx, LANES), F32)], axis=0)
    return cos_t, sin_t


def _mla_weights(w_dq, q_norm, w_uq, w_dkv, kv_norm, w_ukv, heads, nope, rope, vdim):
    ql = w_uq.shape[0]
    kvl = kv_norm.shape[0]
    uq = w_uq.reshape(ql, heads, nope + rope)
    uq_pe = _pad_rope_cols(uq[:, :, nope:]).reshape(ql, heads * LANES)
    wuq = jnp.concatenate([uq[:, :, :nope].reshape(ql, heads * nope), uq_pe], axis=1)
    wdkv = jnp.concatenate([w_dkv[:, :kvl], _pad_rope_cols(w_dkv[:, kvl:])], axis=1)
    ukv = w_ukv.reshape(kvl, heads, nope + vdim)
    wukv = jnp.concatenate([ukv[:, :, :nope].reshape(kvl, heads * nope),
                            ukv[:, :, nope:].reshape(kvl, heads * vdim)], axis=1)
    return (w_dq.astype(BF16), q_norm.reshape(1, ql).astype(F32), wuq.astype(BF16),
            wdkv.astype(BF16), kv_norm.reshape(1, kvl).astype(F32), wukv.astype(BF16))


def kernel(x, c, ctx, c_ctx, ada_w, ada_b, norm1_g, norm2_g, final_norm_g, s5_lam_re, s5_lam_im, s5_log_dt, s5_b_re, s5_b_im, s5_c_re, s5_c_im, s5_d, s5_w_glu, s5_b_glu, mla_w_dq, mla_q_norm, mla_w_uq, mla_w_dkv, mla_kv_norm, mla_w_ukv, mla_w_o, ffn_w1, ffn_w2):
    b, l, d = x.shape
    lc = ctx.shape[1]
    depth = ada_w.shape[0]
    lay = _Layout(b, l, lc, d)

    kvl = mla_kv_norm.shape[1]
    rope = mla_w_dkv.shape[2] - kvl
    heads = (mla_w_uq.shape[2] + mla_w_o.shape[1] - mla_w_ukv.shape[2]) // rope
    vdim = mla_w_o.shape[1] // heads
    nope = mla_w_uq.shape[2] // heads - rope
    assert nope == LANES and vdim == LANES and 4 * (rope // 4) == rope and rope <= LANES
    assert b + 1 <= SUBLANES and s5_b_re.shape[-1] == CHUNK
    scale = (nope + rope) ** -0.5

    xs = jnp.concatenate([_to_chunk_major(x), _to_chunk_major(ctx)], axis=0)

    cc = jnp.concatenate([c, c_ctx[None], jnp.zeros((SUBLANES - b - 1, d), F32)], axis=0)
    mod = _ada_table(cc, ada_w, ada_b).reshape(depth, SUBLANES, 1, 6 * d)
    g1 = norm1_g.reshape(depth, 1, d)
    g2 = norm2_g.reshape(depth, 1, d)
    cos_t, sin_t = _rope_tables(lay, rope)

    for i in range(depth):
        j = i // 2
        if i % 2 == 0:
            tabs = _s5_tables(s5_lam_re[j], s5_lam_im[j], s5_log_dt[j], s5_b_re[j], s5_b_im[j],
                              s5_c_re[j], s5_c_im[j], s5_d[j])
            a = _norm_call(lay, xs, mod, g1, i)
            y_l, y_c = _s5_core_call(lay, a, tabs)
            xs = _glu_call(lay, y_l, y_c, xs, mod, s5_w_glu[j].astype(BF16),
                           s5_b_glu[j].reshape(1, d), i)
        else:
            wts = _mla_weights(mla_w_dq[j], mla_q_norm[j], mla_w_uq[j], mla_w_dkv[j],
                               mla_kv_norm[j], mla_w_ukv[j], heads, nope, rope, vdim)
            q, k, v = _mla_proj_call(lay, xs, mod, g1, cos_t, sin_t, wts, i, heads, scale)
            o = _attn_call(lay, q, k, v, heads)
            xs = _oproj_call(lay, o, xs, mod, mla_w_o[j].astype(BF16), i)
        xs = _ffn_call(lay, xs, mod, g2, ffn_w1[i].astype(BF16), ffn_w2[i].astype(BF16), i)

    out = _final_call(lay, xs, final_norm_g.reshape(1, d))
    return _from_chunk_major(out, b)
```

```python
import functools
import math

import jax
import jax.numpy as jnp
from jax import lax
from jax.experimental import pallas as pl
from jax.experimental.pallas import tpu as pltpu

F32 = jnp.float32
BF16 = jnp.bfloat16

EPS = 1e-6
GRID_W = 64
ROPE_THETA = 10000.0

LANES = 128
SUBLANES = 8
CHUNK = 16
VMEM_LIMIT = 56 * 1024 * 1024

TM_TOKEN = 512
TM_PROJ = 256
TQ_ATTN = 512
ATTN_HEADS = 2
TK_ATTN = 256
TF_FFN = 1024
TN_ADA = 1024


def _params(sem):
    return pltpu.CompilerParams(dimension_semantics=sem, vmem_limit_bytes=VMEM_LIMIT)


def _rms(x):
    return x * lax.rsqrt(jnp.mean(x * x, axis=-1, keepdims=True) + EPS)


def _norm_mod(x, g, sc, sh):
    return (_rms(x) * g) * (1.0 + sc) + sh


def _gelu_tanh(y):
    c = math.sqrt(2.0 / math.pi)
    return 0.5 * y * (1.0 + jnp.tanh(c * (y + 0.044715 * (y * y * y))))


def _ada_kernel(cc_ref, w_ref, b_ref, o_ref):
    cc = cc_ref[...]
    s = cc * jax.nn.sigmoid(cc)
    o_ref[...] = jnp.dot(s, w_ref[...], preferred_element_type=F32,
                         precision=lax.Precision.HIGHEST) + b_ref[...]


def _ada_table(cc, ada_w, ada_b):
    depth, d, n6 = ada_w.shape
    tn = min(TN_ADA, n6)
    return pl.pallas_call(
        _ada_kernel,
        grid=(depth, n6 // tn),
        in_specs=[pl.BlockSpec((SUBLANES, d), lambda l, n: (0, 0)),
                  pl.BlockSpec((None, d, tn), lambda l, n: (l, 0, n)),
                  pl.BlockSpec((None, 1, tn), lambda l, n: (l, 0, n))],
        out_specs=pl.BlockSpec((None, SUBLANES, tn), lambda l, n: (l, 0, n)),
        out_shape=jax.ShapeDtypeStruct((depth, SUBLANES, n6), F32),
        compiler_params=_params(("parallel", "parallel")),
        name="ada_table",
    )(cc, ada_w, ada_b.reshape(depth, 1, n6))


class _Layout:
    def __init__(self, b, l, lc, d):
        self.b, self.l, self.lc, self.d = b, l, lc, d
        self.n_lat = b * l
        self.n = b * (l + lc)

    def mod_row(self, tm):
        n_lat_tiles = self.n_lat // tm
        per_batch = self.l // tm
        ctx_row = self.b

        def row(i):
            return jnp.where(i < n_lat_tiles, i // per_batch, ctx_row)
        return row

    def mod_spec(self, layer, chunk, tm, ngrid=1):
        row = self.mod_row(tm)
        if ngrid == 1:
            return pl.BlockSpec((None, None, 1, self.d), lambda i: (layer, row(i), 0, chunk))
        return pl.BlockSpec((None, None, 1, self.d), lambda i, f: (layer, row(i), 0, chunk))


def _norm_kernel(x_ref, g_ref, sh_ref, sc_ref, o_ref):
    o_ref[...] = _norm_mod(x_ref[...], g_ref[...], sc_ref[...], sh_ref[...]).astype(o_ref.dtype)


def _norm_call(lay, x, mod, g, layer):
    tm, d = TM_TOKEN, lay.d
    return pl.pallas_call(
        _norm_kernel,
        grid=(lay.n // tm,),
        in_specs=[pl.BlockSpec((tm, d), lambda i: (i, 0)),
                  pl.BlockSpec((None, 1, d), lambda i: (layer, 0, 0)),
                  lay.mod_spec(layer, 0, tm), lay.mod_spec(layer, 1, tm)],
        out_specs=pl.BlockSpec((tm, d), lambda i: (i, 0)),
        out_shape=jax.ShapeDtypeStruct((lay.n, d), BF16),
        compiler_params=_params(("parallel",)),
        name="norm_mod",
    )(x, g, mod, mod)


def _ffn_kernel(x_ref, g_ref, sh_ref, sc_ref, gate_ref, w1_ref, w2_ref, o_ref, a_ref, acc_ref):
    f = pl.program_id(1)

    @pl.when(f == 0)
    def _():
        a_ref[...] = _norm_mod(x_ref[...], g_ref[...], sc_ref[...], sh_ref[...]).astype(BF16)
        acc_ref[...] = jnp.zeros_like(acc_ref)

    h = jnp.dot(a_ref[...], w1_ref[...], preferred_element_type=F32)
    h = jnp.maximum(h, 0.0)
    h = (h * h).astype(BF16)
    acc_ref[...] += jnp.dot(h, w2_ref[...], preferred_element_type=F32)

    @pl.when(f == pl.num_programs(1) - 1)
    def _():
        o_ref[...] = x_ref[...] + gate_ref[...] * acc_ref[...]


def _ffn_call(lay, x, mod, g, w1, w2, layer, n_rows):
    tm, d = TM_TOKEN, lay.d
    dff = w1.shape[1]
    tf = min(TF_FFN, dff)
    return pl.pallas_call(
        _ffn_kernel,
        grid=(n_rows // tm, dff // tf),
        in_specs=[pl.BlockSpec((tm, d), lambda i, f: (i, 0)),
                  pl.BlockSpec((None, 1, d), lambda i, f: (layer, 0, 0)),
                  lay.mod_spec(layer, 3, tm, 2), lay.mod_spec(layer, 4, tm, 2),
                  lay.mod_spec(layer, 5, tm, 2),
                  pl.BlockSpec((d, tf), lambda i, f: (0, f)),
                  pl.BlockSpec((tf, d), lambda i, f: (f, 0))],
        out_specs=pl.BlockSpec((tm, d), lambda i, f: (i, 0)),
        out_shape=jax.ShapeDtypeStruct((n_rows, d), F32),
        scratch_shapes=[pltpu.VMEM((tm, d), BF16), pltpu.VMEM((tm, d), F32)],
        compiler_params=_params(("parallel", "arbitrary")),
        name="ffn",
    )(x, g, mod, mod, mod, w1, w2)


def _cmul(ar, ai, br, bi):
    return ar * br - ai * bi, ar * bi + ai * br


def _tile_scan(sr, si, cr, ci, ap_ref, d, reverse):
    rows = lax.broadcasted_iota(jnp.int32, sr.shape, 0)

    def shift(v, k):
        if reverse:
            return jnp.where(rows < SUBLANES - k, pltpu.roll(v, SUBLANES - k, 0), 0.0)
        return jnp.where(rows >= k, pltpu.roll(v, k, 0), 0.0)

    pr, pi = sr, si
    for k in (1, 2, 4):
        akr = ap_ref[d, 0, k:k + 1, :]
        aki = ap_ref[d, 1, k:k + 1, :]
        tr, ti = _cmul(akr, aki, shift(pr, k), shift(pi, k))
        pr, pi = pr + tr, pi + ti
    base = SUBLANES if reverse else 0
    tab_r = ap_ref[d, 0, base:base + SUBLANES, :]
    tab_i = ap_ref[d, 1, base:base + SUBLANES, :]
    tr, ti = _cmul(tab_r, tab_i, cr, ci)
    hr, hi = shift(pr, 1) + tr, shift(pi, 1) + ti
    a8r = ap_ref[d, 0, 2 * SUBLANES:2 * SUBLANES + 1, :]
    a8i = ap_ref[d, 1, 2 * SUBLANES:2 * SUBLANES + 1, :]
    tr, ti = _cmul(a8r, a8i, cr, ci)
    last = 0 if reverse else SUBLANES - 1
    return hr, hi, pr[last:last + 1, :] + tr, pi[last:last + 1, :] + ti


def _split_bf16(v):
    hi = v.astype(BF16)
    return hi, (v - hi.astype(F32)).astype(BF16)


def _s5_core_kernel(*refs, nb, kl, kc):
    al_refs = refs[:nb]
    ac_refs = refs[nb:2 * nb]
    lamt_ref, bt_ref, laml_ref, ct_ref, ap_ref, dsk_ref, yl_ref, yc_ref = refs[2 * nb:2 * nb + 8]
    x_s, w_s, xc_s, wintra_s, bd_s, s_s, y_s = refs[2 * nb + 8:]
    gpl = LANES // CHUNK
    ns = x_s.shape[1] // LANES
    sw = s_s.shape[2]
    half = sw // 2
    lat_rows = nb * kl
    p_rows = ct_ref.shape[2]
    lane_group_p = lax.broadcasted_iota(jnp.int32, (p_rows, LANES), 1) // CHUNK

    for s in range(ns):
        for b in range(nb):
            x_s[b * kl:(b + 1) * kl, s * LANES:(s + 1) * LANES] = al_refs[b][s]
            x_s[lat_rows + b * kc:lat_rows + (b + 1) * kc, s * LANES:(s + 1) * LANES] = ac_refs[b][s]
    x = x_s[...]

    zero_tile = jnp.zeros((CHUNK, LANES), BF16)
    n_lt = half // LANES
    row_i = lax.broadcasted_iota(jnp.int32, (LANES, LANES), 0)
    lane_i = lax.broadcasted_iota(jnp.int32, (LANES, LANES), 1)
    same_group = row_i // CHUNK == lane_i // CHUNK
    even_lo = ((row_i // CHUNK) % 2 == 0) & (lane_i < LANES // 2)
    odd_hi = ((row_i // CHUNK) % 2 == 1) & (lane_i >= LANES // 2)
    k0 = jnp.where(row_i == lane_i, dsk_ref[...], 0.0)
    for d in range(2):
        cur = bt_ref[d, 0]
        lam_a, lam_b = lamt_ref[d, 0], lamt_ref[d, 1]
        for e in range(ns):
            s = ns - 1 - e if d == 0 else e
            swapped = pltpu.roll(cur, LANES // 2, 1)
            hi, lo = _split_bf16(cur)
            xc_s[0, s * LANES:(s + 1) * LANES, :] = hi
            xc_s[1, s * LANES:(s + 1) * LANES, :] = lo
            re_pos = jnp.where(even_lo, cur, jnp.where(odd_hi, swapped, 0.0)).astype(BF16)
            im_pos = jnp.where(even_lo, swapped, jnp.where(odd_hi, cur, 0.0)).astype(BF16)
            for g in range(gpl):
                r0 = g * CHUNK
                tiles = ([re_pos[r0:r0 + CHUNK, :] if lt == g // 2 else zero_tile for lt in range(n_lt)]
                         + [im_pos[r0:r0 + CHUNK, :] if lt == g // 2 else zero_tile for lt in range(n_lt)])
                w_s[d, s * LANES + r0:s * LANES + r0 + CHUNK, :] = jnp.concatenate(tiles, axis=-1)
            cur = lam_a * cur + lam_b * swapped
        s_s[d] = jnp.dot(x, w_s[d], preferred_element_type=F32)
        c_hi, c_lo = _split_bf16(jnp.concatenate([ct_ref[d, 0], -ct_ref[d, 1]], axis=0))
        kall = (jnp.dot(xc_s[0], c_hi, preferred_element_type=F32)
                + jnp.dot(xc_s[0], c_lo, preferred_element_type=F32)
                + jnp.dot(xc_s[1], c_hi, preferred_element_type=F32))
        for e in range(ns):
            s = ns - 1 - e if d == 0 else e
            ktile = jnp.where(same_group, kall[s * LANES:(s + 1) * LANES, :], 0.0)
            if e == 0:
                k0 = k0 + ktile
            else:
                bd_s[ns - 1 + (e if d == 0 else -e)] = ktile.astype(BF16)
    bd_s[ns - 1] = k0.astype(BF16)

    n_ctx_tiles = kc // SUBLANES
    n_lat_tiles = kl // SUBLANES

    def scan_body(i, carry):
        new = []
        for d in range(2):
            for b in range(nb):
                cr, ci = carry[2 * (d * nb + b)], carry[2 * (d * nb + b) + 1]
                ic = i if d == 0 else n_ctx_tiles - 1 - i
                il = i - n_ctx_tiles if d == 0 else n_lat_tiles - 1 - (i - n_ctx_tiles)
                row = jnp.where(i < n_ctx_tiles, lat_rows + b * kc + ic * SUBLANES,
                                b * kl + il * SUBLANES)
                row = pl.multiple_of(row, SUBLANES)
                sr = s_s[d, pl.ds(row, SUBLANES), 0:half]
                si = s_s[d, pl.ds(row, SUBLANES), half:sw]
                hr, hi, cr, ci = _tile_scan(sr, si, cr, ci, ap_ref, d, d == 1)
                s_s[d, pl.ds(row, SUBLANES), 0:half] = hr
                s_s[d, pl.ds(row, SUBLANES), half:sw] = hi
                new += [cr, ci]
        return tuple(new)

    zero_row = jnp.zeros((1, half), F32)
    lax.fori_loop(0, n_ctx_tiles + n_lat_tiles, scan_body, (zero_row,) * (4 * nb))

    for s in range(ns):
        for t in range(ns):
            wintra_s[s * LANES:(s + 1) * LANES, t * LANES:(t + 1) * LANES] = bd_s[t - s + ns - 1]
    y_s[...] = jnp.dot(x, wintra_s[...], preferred_element_type=F32)

    hcols = (ns * LANES) // 2
    for d in range(2):
        lam_r, lam_i = laml_ref[d, 0], laml_ref[d, 1]
        cur_r, cur_i = _cmul(lam_r, lam_i, ct_ref[d, 0], ct_ref[d, 1])
        for e in range(ns):
            t = e if d == 0 else ns - 1 - e
            c0 = t * LANES
            rbase = (c0 // hcols) * sw
            cc = c0 % hcols
            for part, ctile in ((0, cur_r), (1, -cur_i)):
                for g in range(gpl):
                    r0 = rbase + part * half + g * p_rows
                    w_s[d, r0:r0 + p_rows, cc:cc + LANES] = jnp.where(
                        lane_group_p == g, ctile, 0.0).astype(BF16)
            cur_r, cur_i = _cmul(lam_r, lam_i, cur_r, cur_i)
    for d in range(2):
        h = s_s[d].astype(BF16)
        for c in range(2):
            y_s[:, c * hcols:(c + 1) * hcols] += jnp.dot(
                h, w_s[d, c * sw:(c + 1) * sw, :], preferred_element_type=F32)

    for t in range(ns):
        for b in range(nb):
            yl_ref[b * ns + t] = y_s[b * kl:(b + 1) * kl, t * LANES:(t + 1) * LANES]
            yc_ref[b * ns + t] = y_s[lat_rows + b * kc:lat_rows + (b + 1) * kc, t * LANES:(t + 1) * LANES]


def _s5_core_call(lay, a, tabs, layer):
    lamt, bt, laml, ct, apow, dsk = tabs
    nb, d = lay.b, lay.d
    kl, kc = lay.l // CHUNK, lay.lc // CHUNK
    nj = d // LANES
    rows = nb * (kl + kc)
    sw = 2 * (LANES // CHUNK) * ct.shape[-2]
    assert 2 * sw == CHUNK * LANES
    a_l = a.reshape(lay.n // kl, kl, d)
    a_c = a.reshape(lay.n // kc, kc, d)
    ctx_blk0 = (nb * kl) // kc

    def tab_spec(t):
        return pl.BlockSpec((None, 2, t.shape[2], None) + t.shape[4:], lambda j: (layer, 0, 0, j, 0, 0))
    in_specs = ([pl.BlockSpec((CHUNK, kl, LANES), functools.partial(lambda j, b: (b, 0, j), b=b))
                 for b in range(nb)]
                + [pl.BlockSpec((CHUNK, kc, LANES), functools.partial(lambda j, b: (ctx_blk0 + b, 0, j), b=b))
                   for b in range(nb)]
                + [tab_spec(lamt), tab_spec(bt), tab_spec(laml), tab_spec(ct), tab_spec(apow),
                   pl.BlockSpec((None, None, 1, LANES), lambda j: (layer, j, 0, 0))])
    y_l, y_c = pl.pallas_call(
        functools.partial(_s5_core_kernel, nb=nb, kl=kl, kc=kc),
        grid=(nj,),
        in_specs=in_specs,
        out_specs=[pl.BlockSpec((nb * CHUNK, kl, LANES), lambda j: (0, 0, j)),
                   pl.BlockSpec((nb * CHUNK, kc, LANES), lambda j: (0, 0, j))],
        out_shape=[jax.ShapeDtypeStruct((nb * CHUNK, kl, d), F32),
                   jax.ShapeDtypeStruct((nb * CHUNK, kc, d), F32)],
        scratch_shapes=[pltpu.VMEM((rows, CHUNK * LANES), BF16),
                        pltpu.VMEM((2, CHUNK * LANES, sw), BF16),
                        pltpu.VMEM((2, CHUNK * LANES, LANES), BF16),
                        pltpu.VMEM((CHUNK * LANES, CHUNK * LANES), BF16),
                        pltpu.VMEM((2 * CHUNK - 1, LANES, LANES), BF16),
                        pltpu.VMEM((2, rows, sw), F32),
                        pltpu.VMEM((rows, CHUNK * LANES), F32)],
        compiler_params=_params(("parallel",)),
        name="s5_core",
    )(*([a_l] * nb + [a_c] * nb + [lamt, bt, laml, ct, apow, dsk]))
    return y_l.reshape(nb * lay.l, d), y_c.reshape(nb * lay.lc, d)


def _s5_tables(lam_re, lam_im, log_dt, b_re, b_im, c_re, c_im, d_skip):
    nl, _, ng, p = lam_re.shape
    grp = b_re.shape[-1]
    gpl = LANES // grp
    nj = ng // gpl
    dt = jnp.exp(log_dt.astype(F32))[..., None]
    lr, li = lam_re.astype(F32), lam_im.astype(F32)

    mag = jnp.exp(lr * dt)
    lbr, lbi = mag * jnp.cos(li * dt), mag * jnp.sin(li * dt)
    nr = lbr - 1.0
    den = lr * lr + li * li
    fr = (nr * lr + lbi * li) / den
    fi = (lbi * lr - nr * li) / den
    bbr = fr[..., None] * b_re - fi[..., None] * b_im
    bbi = fr[..., None] * b_im + fi[..., None] * b_re

    def rows_gc(re, im):
        v = jnp.concatenate([re, im], axis=-1)[:, :, :, None, :]
        return jnp.broadcast_to(v, (nl, 2, ng, grp, 2 * p)).reshape(nl, 2, nj, gpl * grp, 2 * p)

    def rows_p(v):
        v = v.reshape(nl, 2, nj, gpl, grp, p)
        return jnp.transpose(v, (0, 1, 2, 5, 3, 4)).reshape(nl, 2, nj, p, gpl * grp)

    lamt = jnp.stack([rows_gc(lbr, lbr), rows_gc(-lbi, lbi)], axis=2)
    bt = jnp.concatenate([jnp.swapaxes(bbr, -1, -2), jnp.swapaxes(bbi, -1, -2)], axis=-1)
    bt = bt.reshape(nl, 2, 1, nj, gpl * grp, 2 * p)
    lam_bc = lambda v: jnp.broadcast_to(v[:, :, :, None, :], (nl, 2, ng, grp, p))
    laml = jnp.stack([rows_p(lam_bc(lbr)), rows_p(lam_bc(lbi))], axis=2)
    ct = jnp.stack([rows_p(c_re.astype(F32)), rows_p(c_im.astype(F32))], axis=2)

    m = (CHUNK * jnp.arange(SUBLANES + 1, dtype=F32))[:, None, None, None, None]
    amag = jnp.exp(lr * dt * m)

    def lay_a(v):
        tab = jnp.concatenate([v[:SUBLANES], v[:SUBLANES][::-1], v[SUBLANES:],
                               jnp.zeros((SUBLANES - 1,) + v.shape[1:], F32)], axis=0)
        tab = tab.reshape(3 * SUBLANES, nl, 2, nj, gpl * p)
        return jnp.transpose(tab, (1, 2, 3, 0, 4))
    apow = jnp.stack([lay_a(amag * jnp.cos(li * dt * m)), lay_a(amag * jnp.sin(li * dt * m))], axis=2)
    dsk = d_skip.astype(F32).reshape(nl, nj, 1, LANES)
    return lamt, bt, laml, ct, apow, dsk


def _glu_kernel(yl_ref, yc_ref, x_ref, gate_ref, w_ref, b_ref, o_ref, *, n_lat_tiles):
    i = pl.program_id(0)
    y = jnp.where(i < n_lat_tiles, yl_ref[...], yc_ref[...])
    z = _gelu_tanh(y)
    u = jnp.dot(z.astype(BF16), w_ref[...], preferred_element_type=F32) + b_ref[...]
    o_ref[...] = x_ref[...] + gate_ref[...] * (z * jax.nn.sigmoid(u))


def _glu_call(lay, y_l, y_c, x, mod, w, bias, layer):
    tm, d = TM_TOKEN, lay.d
    nlt = lay.n_lat // tm
    return pl.pallas_call(
        functools.partial(_glu_kernel, n_lat_tiles=nlt),
        grid=(lay.n // tm,),
        in_specs=[pl.BlockSpec((tm, d), lambda i: (jnp.minimum(i, nlt - 1), 0)),
                  pl.BlockSpec((tm, d), lambda i: (jnp.maximum(i - nlt, 0), 0)),
                  pl.BlockSpec((tm, d), lambda i: (i, 0)),
                  lay.mod_spec(layer, 2, tm),
                  pl.BlockSpec((d, d), lambda i: (0, 0)),
                  pl.BlockSpec((1, d), lambda i: (0, 0))],
        out_specs=pl.BlockSpec((tm, d), lambda i: (i, 0)),
        out_shape=jax.ShapeDtypeStruct((lay.n, d), F32),
        compiler_params=_params(("parallel",)),
        name="s5_glu",
    )(y_l, y_c, x, mod, w, bias)


def _rope(v, cos, sin):
    return v * cos + pltpu.roll(v, LANES // 2, 1) * sin


def _mla_proj_kernel(x_ref, g_ref, sh_ref, sc_ref, cos_ref, sin_ref, wdq_ref, qn_ref, wuq_ref,
                     wdkv_ref, kvn_ref, wukv_ref, q_ref, k_ref, v_ref, *, heads, scale):
    a = _norm_mod(x_ref[...], g_ref[...], sc_ref[...], sh_ref[...]).astype(BF16)
    cos, sin = cos_ref[...], sin_ref[...]
    hd = heads * LANES
    cq = jnp.dot(a, wdq_ref[...], preferred_element_type=F32)
    cq = (_rms(cq) * qn_ref[...]).astype(BF16)
    q = jnp.dot(cq, wuq_ref[...], preferred_element_type=F32)
    ckv = jnp.dot(a, wdkv_ref[...], preferred_element_type=F32)
    kvl = ckv.shape[1] - LANES
    ckvn = (_rms(ckv[:, :kvl]) * kvn_ref[...]).astype(BF16)
    kpe = _rope(ckv[:, kvl:], cos, sin).astype(BF16)
    kv = jnp.dot(ckvn, wukv_ref[...], preferred_element_type=F32)
    for h in range(heads):
        lo, hi_ = h * LANES, (h + 1) * LANES
        q_ref[h, :, 0:LANES] = (q[:, lo:hi_] * scale).astype(BF16)
        q_ref[h, :, LANES:2 * LANES] = (_rope(q[:, hd + lo:hd + hi_], cos, sin) * scale).astype(BF16)
        k_ref[h, :, 0:LANES] = kv[:, lo:hi_].astype(BF16)
        k_ref[h, :, LANES:2 * LANES] = kpe
        v_ref[h] = kv[:, hd + lo:hd + hi_].astype(BF16)


def _mla_proj_call(lay, x, mod, g, cos_t, sin_t, wts, layer, heads, scale):
    wdq, qn, wuq, wdkv, kvn, wukv = wts
    tm, d = TM_PROJ, lay.d
    full = lambda arr: pl.BlockSpec(arr.shape, lambda i: (0,) * arr.ndim)
    return pl.pallas_call(
        functools.partial(_mla_proj_kernel, heads=heads, scale=scale),
        grid=(lay.n // tm,),
        in_specs=[pl.BlockSpec((tm, d), lambda i: (i, 0)),
                  pl.BlockSpec((None, 1, d), lambda i: (layer, 0, 0)),
                  lay.mod_spec(layer, 0, tm), lay.mod_spec(layer, 1, tm),
                  pl.BlockSpec((tm, LANES), lambda i: (i, 0)),
                  pl.BlockSpec((tm, LANES), lambda i: (i, 0)),
                  full(wdq), full(qn), full(wuq), full(wdkv), full(kvn), full(wukv)],
        out_specs=[pl.BlockSpec((heads, tm, 2 * LANES), lambda i: (0, i, 0)),
                   pl.BlockSpec((heads, tm, 2 * LANES), lambda i: (0, i, 0)),
                   pl.BlockSpec((heads, tm, LANES), lambda i: (0, i, 0))],
        out_shape=[jax.ShapeDtypeStruct((heads, lay.n, 2 * LANES), BF16),
                   jax.ShapeDtypeStruct((heads, lay.n, 2 * LANES), BF16),
                   jax.ShapeDtypeStruct((heads, lay.n, LANES), BF16)],
        compiler_params=_params(("parallel",)),
        name="mla_proj",
    )(x, g, mod, mod, cos_t, sin_t, wdq, qn, wuq, wdkv, kvn, wukv)


def _attn_kernel(*refs, tk, n_kv):
    q_ref, kc_ref, vc_ref = refs[:3]
    kl_ref, vl_ref = (refs[3], refs[4]) if n_kv else (None, None)
    o_ref = refs[-1]
    nt = (((1,), (1,)), ((), ()))
    hps = q_ref.shape[0]
    qs, ms, ls, accs = [], [], [], []
    for h in range(hps):
        q = q_ref[h]
        s = lax.dot_general(q, kc_ref[h], nt, preferred_element_type=F32)
        m = jnp.max(s, axis=-1, keepdims=True)
        p = jnp.exp2(s - m)
        qs.append(q)
        ms.append(m)
        ls.append(jnp.sum(p, axis=-1, keepdims=True))
        accs.append(jnp.dot(p.astype(BF16), vc_ref[h], preferred_element_type=F32))
    for i in range(n_kv):
        for h in range(hps):
            s = lax.dot_general(qs[h], kl_ref[h, i * tk:(i + 1) * tk, :], nt, preferred_element_type=F32)
            m_new = jnp.maximum(ms[h], jnp.max(s, axis=-1, keepdims=True))
            alpha = jnp.exp2(ms[h] - m_new)
            p = jnp.exp2(s - m_new)
            ls[h] = alpha * ls[h] + jnp.sum(p, axis=-1, keepdims=True)
            accs[h] = alpha * accs[h] + jnp.dot(p.astype(BF16), vl_ref[h, i * tk:(i + 1) * tk, :],
                                                preferred_element_type=F32)
            ms[h] = m_new
    for h in range(hps):
        o_ref[:, h * LANES:(h + 1) * LANES] = (accs[h] / ls[h]).astype(o_ref.dtype)


def _attn_call(lay, q, k, v, heads, need_ctx):
    nb, l, lc = lay.b, lay.l, lay.lc
    hps = ATTN_HEADS
    tq = min(TQ_ATTN, l)
    tk = min(TK_ATTN, l)
    nlq = l // tq
    ctx_blk0 = (nb * l) // lc
    ctx_specs = [pl.BlockSpec((hps, lc, 2 * LANES), lambda b, h, qi: (h, ctx_blk0 + b, 0)),
                 pl.BlockSpec((hps, lc, LANES), lambda b, h, qi: (h, ctx_blk0 + b, 0))]
    out_shape = jax.ShapeDtypeStruct((lay.n, heads * LANES), BF16)
    o = pl.pallas_call(
        functools.partial(_attn_kernel, tk=tk, n_kv=l // tk),
        grid=(nb, heads // hps, nlq),
        in_specs=[pl.BlockSpec((hps, tq, 2 * LANES), lambda b, h, qi: (h, b * nlq + qi, 0))] + ctx_specs
                 + [pl.BlockSpec((hps, l, 2 * LANES), lambda b, h, qi: (h, b, 0)),
                    pl.BlockSpec((hps, l, LANES), lambda b, h, qi: (h, b, 0))],
        out_specs=pl.BlockSpec((tq, hps * LANES), lambda b, h, qi: (b * nlq + qi, h)),
        out_shape=out_shape,
        compiler_params=_params(("parallel", "parallel", "arbitrary")),
        name="mla_attn",
    )(q, k, v, k, v)
    if not need_ctx:
        return o
    return pl.pallas_call(
        functools.partial(_attn_kernel, tk=tk, n_kv=0),
        grid=(nb, heads // hps, 1),
        in_specs=[pl.BlockSpec((hps, lc, 2 * LANES), lambda b, h, qi: (h, ctx_blk0 + b, 0))] + ctx_specs
                 + [pl.BlockSpec(memory_space=pl.ANY)],
        out_specs=pl.BlockSpec((lc, hps * LANES), lambda b, h, qi: (ctx_blk0 + b, h)),
        out_shape=out_shape,
        input_output_aliases={3: 0},
        compiler_params=_params(("parallel", "parallel", "arbitrary")),
        name="mla_attn_ctx",
    )(q, k, v, o)


def _oproj_kernel(o_ref, x_ref, gate_ref, w_ref, out_ref):
    out_ref[...] = x_ref[...] + gate_ref[...] * jnp.dot(o_ref[...], w_ref[...], preferred_element_type=F32)


def _oproj_call(lay, o, x, mod, w, layer, n_rows):
    tm, d = TM_TOKEN, lay.d
    return pl.pallas_call(
        _oproj_kernel,
        grid=(n_rows // tm,),
        in_specs=[pl.BlockSpec((tm, o.shape[1]), lambda i: (i, 0)),
                  pl.BlockSpec((tm, d), lambda i: (i, 0)),
                  lay.mod_spec(layer, 2, tm),
                  pl.BlockSpec(w.shape, lambda i: (0, 0))],
        out_specs=pl.BlockSpec((tm, d), lambda i: (i, 0)),
        out_shape=jax.ShapeDtypeStruct((n_rows, d), F32),
        compiler_params=_params(("parallel",)),
        name="mla_oproj",
    )(o, x, mod, w)


def _final_kernel(x_ref, g_ref, o_ref):
    o_ref[...] = _rms(x_ref[...]) * g_ref[...]


def _final_call(lay, x, g):
    tm, d = TM_TOKEN, lay.d
    return pl.pallas_call(
        _final_kernel,
        grid=(lay.n_lat // tm,),
        in_specs=[pl.BlockSpec((tm, d), lambda i: (i, 0)),
                  pl.BlockSpec((1, d), lambda i: (0, 0))],
        out_specs=pl.BlockSpec((tm, d), lambda i: (i, 0)),
        out_shape=jax.ShapeDtypeStruct((lay.n_lat, d), F32),
        compiler_params=_params(("parallel",)),
        name="final_norm",
    )(x, g)


def _to_chunk_major(t):
    b, n, d = t.shape
    return jnp.transpose(t.reshape(b, n // CHUNK, CHUNK, d), (0, 2, 1, 3)).reshape(b * n, d)


def _from_chunk_major(t, b):
    n, d = t.shape[0] // b, t.shape[1]
    return jnp.transpose(t.reshape(b, CHUNK, n // CHUNK, d), (0, 2, 1, 3)).reshape(b, n, d)


def _pad_rope_cols(w):
    nf = w.shape[-1] // 4
    w4 = w.reshape(w.shape[:-1] + (2, 2, nf))
    z = jnp.zeros(w.shape[:-1] + (LANES // 2 - 2 * nf,), w.dtype)
    x1 = w4[..., 0, :].reshape(w.shape[:-1] + (2 * nf,))
    x2 = w4[..., 1, :].reshape(w.shape[:-1] + (2 * nf,))
    return jnp.concatenate([x1, z, x2, z], axis=-1)


def _rope_tables(lay, rope_dim):
    nf = rope_dim // 4
    pos = jnp.arange(lay.l)
    inv = ROPE_THETA ** (-jnp.arange(nf, dtype=F32) / nf)
    row = (pos // GRID_W).astype(F32)
    col = (pos % GRID_W).astype(F32)
    ang = jnp.concatenate([row[:, None] * inv, col[:, None] * inv], axis=1)
    z = jnp.zeros((lay.l, LANES // 2 - 2 * nf), F32)
    cos_l = jnp.concatenate([jnp.cos(ang), z, jnp.cos(ang), z], axis=1)
    sin_l = jnp.concatenate([-jnp.sin(ang), z, jnp.sin(ang), z], axis=1)
    cos_l = _to_chunk_major(jnp.broadcast_to(cos_l[None], (lay.b, lay.l, LANES)))
    sin_l = _to_chunk_major(jnp.broadcast_to(sin_l[None], (lay.b, lay.l, LANES)))
    n_ctx = lay.b * lay.lc
    cos_t = jnp.concatenate([cos_l, jnp.ones((n_ctx, LANES), F32)], axis=0)
    sin_t = jnp.concatenate([sin_l, jnp.zeros((n_ctx, LANES), F32)], axis=0)
    return cos_t, sin_t


def _mla_weights(w_dq, q_norm, w_uq, w_dkv, kv_norm, w_ukv, heads, nope, rope, vdim):
    ql = w_uq.shape[0]
    kvl = kv_norm.shape[0]
    uq = w_uq.reshape(ql, heads, nope + rope)
    uq_pe = _pad_rope_cols(uq[:, :, nope:]).reshape(ql, heads * LANES)
    wuq = jnp.concatenate([uq[:, :, :nope].reshape(ql, heads * nope), uq_pe], axis=1)
    wdkv = jnp.concatenate([w_dkv[:, :kvl], _pad_rope_cols(w_dkv[:, kvl:])], axis=1)
    ukv = w_ukv.reshape(kvl, heads, nope + vdim)
    wukv = jnp.concatenate([ukv[:, :, :nope].reshape(kvl, heads * nope),
                            ukv[:, :, nope:].reshape(kvl, heads * vdim)], axis=1)
    return (w_dq.astype(BF16), q_norm.reshape(1, ql).astype(F32), wuq.astype(BF16),
            wdkv.astype(BF16), kv_norm.reshape(1, kvl).astype(F32), wukv.astype(BF16))


def kernel(x, c, ctx, c_ctx, ada_w, ada_b, norm1_g, norm2_g, final_norm_g, s5_lam_re, s5_lam_im, s5_log_dt, s5_b_re, s5_b_im, s5_c_re, s5_c_im, s5_d, s5_w_glu, s5_b_glu, mla_w_dq, mla_q_norm, mla_w_uq, mla_w_dkv, mla_kv_norm, mla_w_ukv, mla_w_o, ffn_w1, ffn_w2):
    b, l, d = x.shape
    lc = ctx.shape[1]
    depth = ada_w.shape[0]
    lay = _Layout(b, l, lc, d)

    kvl = mla_kv_norm.shape[1]
    rope = mla_w_dkv.shape[2] - kvl
    heads = (mla_w_uq.shape[2] + mla_w_o.shape[1] - mla_w_ukv.shape[2]) // rope
    vdim = mla_w_o.shape[1] // heads
    nope = mla_w_uq.shape[2] // heads - rope
    assert nope == LANES and vdim == LANES and 4 * (rope // 4) == rope and rope <= LANES
    assert b + 1 <= SUBLANES and s5_b_re.shape[-1] == CHUNK
    scale = (nope + rope) ** -0.5 * math.log2(math.e)

    xs =jnp.concatenate([_to_chunk_major(x), _to_chunk_major(ctx)], axis=0)

    cc = jnp.concatenate([c, c_ctx[None], jnp.zeros((SUBLANES - b - 1, d), F32)], axis=0)
    mod = _ada_table(cc, ada_w, ada_b).reshape(depth, SUBLANES, 1, 6 * d)
    g1 = norm1_g.reshape(depth, 1, d)
    g2 = norm2_g.reshape(depth, 1, d)
    cos_t, sin_t = _rope_tables(lay, rope)

    tabs = _s5_tables(s5_lam_re, s5_lam_im, s5_log_dt, s5_b_re, s5_b_im, s5_c_re, s5_c_im, s5_d)

    for i in range(depth):
        j = i // 2
        n_rows = lay.n_lat if i == depth - 1 else lay.n
        if i % 2 == 0:
            a = _norm_call(lay, xs, mod, g1, i)
            y_l, y_c = _s5_core_call(lay, a, tabs, j)
            xs = _glu_call(lay, y_l, y_c, xs, mod, s5_w_glu[j].astype(BF16),
                           s5_b_glu[j].reshape(1, d), i)
        else:
            wts = _mla_weights(mla_w_dq[j], mla_q_norm[j], mla_w_uq[j], mla_w_dkv[j],
                               mla_kv_norm[j], mla_w_ukv[j], heads, nope, rope, vdim)
            q, k, v = _mla_proj_call(lay, xs, mod, g1, cos_t, sin_t, wts, i, heads, scale)
            o = _attn_call(lay, q, k, v, heads, n_rows == lay.n)
            xs = _oproj_call(lay, o, xs, mod, mla_w_o[j].astype(BF16), i, n_rows)
        xs = _ffn_call(lay, xs, mod, g2, ffn_w1[i].astype(BF16), ffn_w2[i].astype(BF16), i, n_rows)

    out = _final_call(lay, xs, final_norm_g.reshape(1, d))
    return _from_chunk_major(out, b)
```

```python
import functools
import math

import jax
import jax.numpy as jnp
from jax import lax
from jax.experimental import pallas as pl
from jax.experimental.pallas import tpu as pltpu

F32 = jnp.float32
BF16 = jnp.bfloat16

EPS = 1e-6
GRID_W = 64
ROPE_THETA = 10000.0

LANES = 128
SUBLANES = 8
CHUNK = 16
VMEM_LIMIT = 56 * 1024 * 1024

TM_TOKEN = 512
TM_PROJ = 256
TQ_ATTN = 1024
ATTN_HEADS = 2
TK_ATTN = 256
TF_FFN = 1024
TN_ADA = 1024


def _params(sem):
    return pltpu.CompilerParams(dimension_semantics=sem, vmem_limit_bytes=VMEM_LIMIT)


def _rms(x):
    return x * lax.rsqrt(jnp.mean(x * x, axis=-1, keepdims=True) + EPS)


def _norm_mod(x, g, sc, sh):
    return (_rms(x) * g) * (1.0 + sc) + sh


def _gelu_tanh(y):
    c = math.sqrt(2.0 / math.pi)
    return 0.5 * y * (1.0 + jnp.tanh(c * (y + 0.044715 * (y * y * y))))


def _ada_kernel(cb_ref, w_ref, b_ref, o_ref, sb_s):
    n_rows, d, _ = cb_ref.shape
    n_lt = w_ref.shape[1] // LANES
    @pl.when(pl.program_id(1) == 0)
    def _():
        cb = cb_ref[...]
        sb_s[...] = cb * jax.nn.sigmoid(cb)
    rows = lax.broadcasted_iota(jnp.int32, (SUBLANES, 1), 0)

    def body(kb, accs):
        k0 = pl.multiple_of(kb * SUBLANES, SUBLANES)
        w = w_ref[pl.ds(k0, SUBLANES), :]
        return tuple(acc + w * jnp.concatenate([sb_s[r, pl.ds(k0, SUBLANES), :]] * n_lt, axis=1)
                     for r, acc in enumerate(accs))
    accs = lax.fori_loop(0, d // SUBLANES, body, (jnp.zeros(o_ref.shape, F32),) * n_rows, unroll=4)
    out = jnp.broadcast_to(b_ref[...], o_ref.shape)
    for r, acc in enumerate(accs):
        out = out + jnp.where(rows == r, jnp.sum(acc, axis=0, keepdims=True), 0.0)
    o_ref[...] = out


def _ada_table(cc, ada_w, ada_b):
    depth, d, n6 = ada_w.shape
    n_rows = cc.shape[0]
    tn = min(TN_ADA, n6)
    cb = jnp.broadcast_to(cc[:, :, None], (n_rows, d, LANES))
    return pl.pallas_call(
        _ada_kernel,
        grid=(depth, n6 // tn),
        in_specs=[pl.BlockSpec((n_rows, d, LANES), lambda l, n: (0, 0, 0)),
                  pl.BlockSpec((None, d, tn), lambda l, n: (l, 0, n)),
                  pl.BlockSpec((None, 1, tn), lambda l, n: (l, 0, n))],
        out_specs=pl.BlockSpec((None, SUBLANES, tn), lambda l, n: (l, 0, n)),
        out_shape=jax.ShapeDtypeStruct((depth, SUBLANES, n6), F32),
        scratch_shapes=[pltpu.VMEM((n_rows, d, LANES), F32)],
        compiler_params=_params(("parallel", "arbitrary")),
        name="ada_table",
    )(cb, ada_w, ada_b.reshape(depth, 1, n6))


class _Layout:
    def __init__(self, b, l, lc, d):
        self.b, self.l, self.lc, self.d = b, l, lc, d
        self.n_lat = b * l
        self.n = b * (l + lc)

    def mod_row(self, tm):
        n_lat_tiles = self.n_lat // tm
        per_batch = self.l // tm
        ctx_row = self.b

        def row(i):
            return jnp.where(i < n_lat_tiles, i // per_batch, ctx_row)
        return row

    def mod_spec(self, layer, chunk, tm, ngrid=1):
        row = self.mod_row(tm)
        if ngrid == 1:
            return pl.BlockSpec((None, None, 1, self.d), lambda i: (layer, row(i), 0, chunk))
        return pl.BlockSpec((None, None, 1, self.d), lambda i, f: (layer, row(i), 0, chunk))


def _norm_kernel(x_ref, g_ref, sh_ref, sc_ref, o_ref):
    o_ref[...] = _norm_mod(x_ref[...], g_ref[...], sc_ref[...], sh_ref[...]).astype(o_ref.dtype)


def _norm_call(lay, x, mod, g, layer):
    tm, d = TM_TOKEN, lay.d
    return pl.pallas_call(
        _norm_kernel,
        grid=(lay.n // tm,),
        in_specs=[pl.BlockSpec((tm, d), lambda i: (i, 0)),
                  pl.BlockSpec((None, 1, d), lambda i: (layer, 0, 0)),
                  lay.mod_spec(layer, 0, tm), lay.mod_spec(layer, 1, tm)],
        out_specs=pl.BlockSpec((tm, d), lambda i: (i, 0)),
        out_shape=jax.ShapeDtypeStruct((lay.n, d), BF16),
        compiler_params=_params(("parallel",)),
        name="norm_mod",
    )(x, g, mod, mod)


def _ffn_kernel(x_ref, g_ref, sh_ref, sc_ref, gate_ref, w1_ref, w2_ref, o_ref, a_ref, acc_ref):
    f = pl.program_id(1)

    @pl.when(f == 0)
    def _():
        a_ref[...] = _norm_mod(x_ref[...], g_ref[...], sc_ref[...], sh_ref[...]).astype(BF16)
        acc_ref[...] = jnp.zeros_like(acc_ref)

    h = jnp.dot(a_ref[...], w1_ref[...], preferred_element_type=F32)
    h = jnp.maximum(h, 0.0)
    h = (h * h).astype(BF16)
    acc_ref[...] += jnp.dot(h, w2_ref[...], preferred_element_type=F32)

    @pl.when(f == pl.num_programs(1) - 1)
    def _():
        o_ref[...] = x_ref[...] + gate_ref[...] * acc_ref[...]


def _ffn_call(lay, x, mod, g, w1, w2, layer, n_rows):
    tm, d = TM_TOKEN, lay.d
    dff = w1.shape[2]
    tf = min(TF_FFN, dff)
    return pl.pallas_call(
        _ffn_kernel,
        grid=(n_rows // tm, dff // tf),
        in_specs=[pl.BlockSpec((tm, d), lambda i, f: (i, 0)),
                  pl.BlockSpec((None, 1, d), lambda i, f: (layer, 0, 0)),
                  lay.mod_spec(layer, 3, tm, 2), lay.mod_spec(layer, 4, tm, 2),
                  lay.mod_spec(layer, 5, tm, 2),
                  pl.BlockSpec((None, d, tf), lambda i, f: (layer, 0, f)),
                  pl.BlockSpec((None, tf, d), lambda i, f: (layer, f, 0))],
        out_specs=pl.BlockSpec((tm, d), lambda i, f: (i, 0)),
        out_shape=jax.ShapeDtypeStruct((n_rows, d), F32),
        scratch_shapes=[pltpu.VMEM((tm, d), BF16), pltpu.VMEM((tm, d), F32)],
        compiler_params=_params(("parallel", "arbitrary")),
        name="ffn",
    )(x, g, mod, mod, mod, w1, w2)


def _cmul(ar, ai, br, bi):
    return ar * br - ai * bi, ar * bi + ai * br


def _tile_scan(sr, si, cr, ci, ap_ref, d, reverse):
    rows = lax.broadcasted_iota(jnp.int32, sr.shape, 0)

    def shift(v, k):
        if reverse:
            return jnp.where(rows < SUBLANES - k, pltpu.roll(v, SUBLANES - k, 0), 0.0)
        return jnp.where(rows >= k, pltpu.roll(v, k, 0), 0.0)

    pr, pi = sr, si
    for k in (1, 2, 4):
        akr = ap_ref[d, 0, k:k + 1, :]
        aki = ap_ref[d, 1, k:k + 1, :]
        tr, ti = _cmul(akr, aki, shift(pr, k), shift(pi, k))
        pr, pi = pr + tr, pi + ti
    base = SUBLANES if reverse else 0
    tab_r = ap_ref[d, 0, base:base + SUBLANES, :]
    tab_i = ap_ref[d, 1, base:base + SUBLANES, :]
    tr, ti = _cmul(tab_r, tab_i, cr, ci)
    hr, hi = shift(pr, 1) + tr, shift(pi, 1) + ti
    a8r = ap_ref[d, 0, 2 * SUBLANES:2 * SUBLANES + 1, :]
    a8i = ap_ref[d, 1, 2 * SUBLANES:2 * SUBLANES + 1, :]
    tr, ti = _cmul(a8r, a8i, cr, ci)
    last = 0 if reverse else SUBLANES - 1
    return hr, hi, pr[last:last + 1, :] + tr, pi[last:last + 1, :] + ti


def _split_bf16(v):
    hi = v.astype(BF16)
    return hi, (v - hi.astype(F32)).astype(BF16)


def _s5_core_kernel(*refs, nb, kl, kc):
    al_refs = refs[:nb]
    ac_refs = refs[nb:2 * nb]
    lamt_ref, bt_ref, laml_ref, ct_ref, ap_ref, dsk_ref, yl_ref, yc_ref = refs[2 * nb:2 * nb + 8]
    x_s, w_s, xc_s, wintra_s, bd_s, s_s, y_s = refs[2 * nb + 8:]
    gpl = LANES // CHUNK
    ns = x_s.shape[1] // LANES
    sw = s_s.shape[2]
    half = sw // 2
    lat_rows = nb * kl
    p_rows = ct_ref.shape[2]
    lane_group_p = lax.broadcasted_iota(jnp.int32, (p_rows, LANES), 1) // CHUNK

    for s in range(ns):
        for b in range(nb):
            x_s[b * kl:(b + 1) * kl, s * LANES:(s + 1) * LANES] = al_refs[b][s]
            x_s[lat_rows + b * kc:lat_rows + (b + 1) * kc, s * LANES:(s + 1) * LANES] = ac_refs[b][s]
    x = x_s[...]

    zero_tile = jnp.zeros((CHUNK, LANES), BF16)
    n_lt = half // LANES
    row_i = lax.broadcasted_iota(jnp.int32, (LANES, LANES), 0)
    lane_i = lax.broadcasted_iota(jnp.int32, (LANES, LANES), 1)
    same_group = row_i // CHUNK == lane_i // CHUNK
    even_lo = ((row_i // CHUNK) % 2 == 0) & (lane_i < LANES // 2)
    odd_hi = ((row_i // CHUNK) % 2 == 1) & (lane_i >= LANES // 2)
    k0 = jnp.where(row_i == lane_i, dsk_ref[...], 0.0)
    for d in range(2):
        cur = bt_ref[d, 0]
        lam_a, lam_b = lamt_ref[d, 0], lamt_ref[d, 1]
        for e in range(ns):
            s = ns - 1 - e if d == 0 else e
            swapped = pltpu.roll(cur, LANES // 2, 1)
            hi, lo = _split_bf16(cur)
            xc_s[0, s * LANES:(s + 1) * LANES, :] = hi
            xc_s[1, s * LANES:(s + 1) * LANES, :] = lo
            re_pos = jnp.where(even_lo, cur, jnp.where(odd_hi, swapped, 0.0)).astype(BF16)
            im_pos = jnp.where(even_lo, swapped, jnp.where(odd_hi, cur, 0.0)).astype(BF16)
            for g in range(gpl):
                r0 = g * CHUNK
                tiles = ([re_pos[r0:r0 + CHUNK, :] if lt == g // 2 else zero_tile for lt in range(n_lt)]
                         + [im_pos[r0:r0 + CHUNK, :] if lt == g // 2 else zero_tile for lt in range(n_lt)])
                w_s[d, s * LANES + r0:s * LANES + r0 + CHUNK, :] = jnp.concatenate(tiles, axis=-1)
            cur = lam_a * cur + lam_b * swapped
        s_s[d] = jnp.dot(x, w_s[d], preferred_element_type=F32)
        c_hi, c_lo = _split_bf16(jnp.concatenate([ct_ref[d, 0], -ct_ref[d, 1]], axis=0))
        kall = (jnp.dot(xc_s[0], c_hi, preferred_element_type=F32)
                + jnp.dot(xc_s[0], c_lo, preferred_element_type=F32)
                + jnp.dot(xc_s[1], c_hi, preferred_element_type=F32))
        for e in range(ns):
            s = ns - 1 - e if d == 0 else e
            ktile = jnp.where(same_group, kall[s * LANES:(s + 1) * LANES, :], 0.0)
            if e == 0:
                k0 = k0 + ktile
            else:
                bd_s[ns - 1 + (e if d == 0 else -e)] = ktile.astype(BF16)
    bd_s[ns - 1] = k0.astype(BF16)

    n_ctx_tiles = kc // SUBLANES
    n_lat_tiles = kl // SUBLANES

    def scan_body(i, carry):
        new = []
        for d in range(2):
            for b in range(nb):
                cr, ci = carry[2 * (d * nb + b)], carry[2 * (d * nb + b) + 1]
                ic = i if d == 0 else n_ctx_tiles - 1 - i
                il = i - n_ctx_tiles if d == 0 else n_lat_tiles - 1 - (i - n_ctx_tiles)
                row = jnp.where(i < n_ctx_tiles, lat_rows + b * kc + ic * SUBLANES,
                                b * kl + il * SUBLANES)
                row = pl.multiple_of(row, SUBLANES)
                sr = s_s[d, pl.ds(row, SUBLANES), 0:half]
                si = s_s[d, pl.ds(row, SUBLANES), half:sw]
                hr, hi, cr, ci = _tile_scan(sr, si, cr, ci, ap_ref, d, d == 1)
                s_s[d, pl.ds(row, SUBLANES), 0:half] = hr
                s_s[d, pl.ds(row, SUBLANES), half:sw] = hi
                new += [cr, ci]
        return tuple(new)

    zero_row = jnp.zeros((1, half), F32)
    lax.fori_loop(0, n_ctx_tiles + n_lat_tiles, scan_body, (zero_row,) * (4 * nb))

    for s in range(ns):
        for t in range(ns):
            wintra_s[s * LANES:(s + 1) * LANES, t * LANES:(t + 1) * LANES] = bd_s[t - s + ns - 1]
    y_s[...] = jnp.dot(x, wintra_s[...], preferred_element_type=F32)

    hcols = (ns * LANES) // 2
    for d in range(2):
        lam_r, lam_i = laml_ref[d, 0], laml_ref[d, 1]
        cur_r, cur_i = _cmul(lam_r, lam_i, ct_ref[d, 0], ct_ref[d, 1])
        for e in range(ns):
            t = e if d == 0 else ns - 1 - e
            c0 = t * LANES
            rbase = (c0 // hcols) * sw
            cc = c0 % hcols
            for part, ctile in ((0, cur_r), (1, -cur_i)):
                for g in range(gpl):
                    r0 = rbase + part * half + g * p_rows
                    w_s[d, r0:r0 + p_rows, cc:cc + LANES] = jnp.where(
                        lane_group_p == g, ctile, 0.0).astype(BF16)
            cur_r, cur_i = _cmul(lam_r, lam_i, cur_r, cur_i)
    for d in range(2):
        h = s_s[d].astype(BF16)
        for c in range(2):
            y_s[:, c * hcols:(c + 1) * hcols] += jnp.dot(
                h, w_s[d, c * sw:(c + 1) * sw, :], preferred_element_type=F32)

    for t in range(ns):
        for b in range(nb):
            yl_ref[b * ns + t] = y_s[b * kl:(b + 1) * kl, t * LANES:(t + 1) * LANES]
            yc_ref[b * ns + t] = y_s[lat_rows + b * kc:lat_rows + (b + 1) * kc, t * LANES:(t + 1) * LANES]


def _s5_core_call(lay, a, tabs, layer):
    lamt, bt, laml, ct, apow, dsk = tabs
    nb, d = lay.b, lay.d
    kl, kc = lay.l // CHUNK, lay.lc // CHUNK
    nj = d // LANES
    rows = nb * (kl + kc)
    sw = 2 * (LANES // CHUNK) * ct.shape[-2]
    assert 2 * sw == CHUNK * LANES
    a_l = a.reshape(lay.n // kl, kl, d)
    a_c = a.reshape(lay.n // kc, kc, d)
    ctx_blk0 = (nb * kl) // kc

    def tab_spec(t):
        return pl.BlockSpec((None, 2, t.shape[2], None) + t.shape[4:], lambda j: (layer, 0, 0, j, 0, 0))
    in_specs = ([pl.BlockSpec((CHUNK, kl, LANES), functools.partial(lambda j, b: (b, 0, j), b=b))
                 for b in range(nb)]
                + [pl.BlockSpec((CHUNK, kc, LANES), functools.partial(lambda j, b: (ctx_blk0 + b, 0, j), b=b))
                   for b in range(nb)]
                + [tab_spec(lamt), tab_spec(bt), tab_spec(laml), tab_spec(ct), tab_spec(apow),
                   pl.BlockSpec((None, None, 1, LANES), lambda j: (layer, j, 0, 0))])
    y_l, y_c = pl.pallas_call(
        functools.partial(_s5_core_kernel, nb=nb, kl=kl, kc=kc),
        grid=(nj,),
        in_specs=in_specs,
        out_specs=[pl.BlockSpec((nb * CHUNK, kl, LANES), lambda j: (0, 0, j)),
                   pl.BlockSpec((nb * CHUNK, kc, LANES), lambda j: (0, 0, j))],
        out_shape=[jax.ShapeDtypeStruct((nb * CHUNK, kl, d), F32),
                   jax.ShapeDtypeStruct((nb * CHUNK, kc, d), F32)],
        scratch_shapes=[pltpu.VMEM((rows, CHUNK * LANES), BF16),
                        pltpu.VMEM((2, CHUNK * LANES, sw), BF16),
                        pltpu.VMEM((2, CHUNK * LANES, LANES), BF16),
                        pltpu.VMEM((CHUNK * LANES, CHUNK * LANES), BF16),
                        pltpu.VMEM((2 * CHUNK - 1, LANES, LANES), BF16),
                        pltpu.VMEM((2, rows, sw), F32),
                        pltpu.VMEM((rows, CHUNK * LANES), F32)],
        compiler_params=_params(("parallel",)),
        name="s5_core",
    )(*([a_l] * nb + [a_c] * nb + [lamt, bt, laml, ct, apow, dsk]))
    return y_l.reshape(nb * lay.l, d), y_c.reshape(nb * lay.lc, d)


def _s5_tables(lam_re, lam_im, log_dt, b_re, b_im, c_re, c_im, d_skip):
    nl, _, ng, p = lam_re.shape
    grp = b_re.shape[-1]
    gpl = LANES // grp
    nj = ng // gpl
    dt = jnp.exp(log_dt.astype(F32))[..., None]
    lr, li = lam_re.astype(F32), lam_im.astype(F32)

    mag = jnp.exp(lr * dt)
    lbr, lbi = mag * jnp.cos(li * dt), mag * jnp.sin(li * dt)
    nr = lbr - 1.0
    den = lr * lr + li * li
    fr = (nr * lr + lbi * li) / den
    fi = (lbi * lr - nr * li) / den
    bbr = fr[..., None] * b_re - fi[..., None] * b_im
    bbi = fr[..., None] * b_im + fi[..., None] * b_re

    def rows_gc(re, im):
        v = jnp.concatenate([re, im], axis=-1)[:, :, :, None, :]
        return jnp.broadcast_to(v, (nl, 2, ng, grp, 2 * p)).reshape(nl, 2, nj, gpl * grp, 2 * p)

    def rows_p(v):
        v = v.reshape(nl, 2, nj, gpl, grp, p)
        return jnp.transpose(v, (0, 1, 2, 5, 3, 4)).reshape(nl, 2, nj, p, gpl * grp)

    lamt = jnp.stack([rows_gc(lbr, lbr), rows_gc(-lbi, lbi)], axis=2)
    bt = jnp.concatenate([jnp.swapaxes(bbr, -1, -2), jnp.swapaxes(bbi, -1, -2)], axis=-1)
    bt = bt.reshape(nl, 2, 1, nj, gpl * grp, 2 * p)
    lam_bc = lambda v: jnp.broadcast_to(v[:, :, :, None, :], (nl, 2, ng, grp, p))
    laml = jnp.stack([rows_p(lam_bc(lbr)), rows_p(lam_bc(lbi))], axis=2)
    ct = jnp.stack([rows_p(c_re.astype(F32)), rows_p(c_im.astype(F32))], axis=2)

    m = (CHUNK * jnp.arange(SUBLANES + 1, dtype=F32))[:, None, None, None, None]
    amag = jnp.exp(lr * dt * m)

    def lay_a(v):
        tab = jnp.concatenate([v[:SUBLANES], v[:SUBLANES][::-1], v[SUBLANES:],
                               jnp.zeros((SUBLANES - 1,) + v.shape[1:], F32)], axis=0)
        tab = tab.reshape(3 * SUBLANES, nl, 2, nj, gpl * p)
        return jnp.transpose(tab, (1, 2, 3, 0, 4))
    apow = jnp.stack([lay_a(amag * jnp.cos(li * dt * m)), lay_a(amag * jnp.sin(li * dt * m))], axis=2)
    dsk = d_skip.astype(F32).reshape(nl, nj, 1, LANES)
    return lamt, bt, laml, ct, apow, dsk


def _glu_kernel(yl_ref, yc_ref, x_ref, gate_ref, w_ref, b_ref, o_ref, *, n_lat_tiles):
    i = pl.program_id(0)
    y = jnp.where(i < n_lat_tiles, yl_ref[...], yc_ref[...])
    z = _gelu_tanh(y)
    u = jnp.dot(z.astype(BF16), w_ref[...], preferred_element_type=F32) + b_ref[...]
    o_ref[...] = x_ref[...] + gate_ref[...] * (z * jax.nn.sigmoid(u))


def _glu_call(lay, y_l, y_c, x, mod, w, bias, layer, j):
    tm, d = TM_TOKEN, lay.d
    nlt = lay.n_lat // tm
    return pl.pallas_call(
        functools.partial(_glu_kernel, n_lat_tiles=nlt),
        grid=(lay.n // tm,),
        in_specs=[pl.BlockSpec((tm, d), lambda i: (jnp.minimum(i, nlt - 1), 0)),
                  pl.BlockSpec((tm, d), lambda i: (jnp.maximum(i - nlt, 0), 0)),
                  pl.BlockSpec((tm, d), lambda i: (i, 0)),
                  lay.mod_spec(layer, 2, tm),
                  pl.BlockSpec((None, d, d), lambda i: (j, 0, 0)),
                  pl.BlockSpec((None, 1, d), lambda i: (j, 0, 0))],
        out_specs=pl.BlockSpec((tm, d), lambda i: (i, 0)),
        out_shape=jax.ShapeDtypeStruct((lay.n, d), F32),
        compiler_params=_params(("parallel",)),
        name="s5_glu",
    )(y_l, y_c, x, mod, w, bias)


def _rope(v, cos, sin):
    return v * cos + pltpu.roll(v, LANES // 2, 1) * sin


def _mla_proj_kernel(x_ref, g_ref, sh_ref, sc_ref, cos_ref, sin_ref, wdq_ref, qn_ref, wuq_ref,
                     wdkv_ref, kvn_ref, wukv_ref, q_ref, k_ref, v_ref, *, heads, scale):
    a = _norm_mod(x_ref[...], g_ref[...], sc_ref[...], sh_ref[...]).astype(BF16)
    cos, sin = cos_ref[...], sin_ref[...]
    hd = heads * LANES
    cq = jnp.dot(a, wdq_ref[...], preferred_element_type=F32)
    cq = (_rms(cq) * qn_ref[...]).astype(BF16)
    q = jnp.dot(cq, wuq_ref[...], preferred_element_type=F32)
    ckv = jnp.dot(a, wdkv_ref[...], preferred_element_type=F32)
    kvl = ckv.shape[1] - LANES
    ckvn = (_rms(ckv[:, :kvl]) * kvn_ref[...]).astype(BF16)
    kpe = _rope(ckv[:, kvl:], cos, sin).astype(BF16)
    kv = jnp.dot(ckvn, wukv_ref[...], preferred_element_type=F32)
    for h in range(heads):
        lo, hi_ = h * LANES, (h + 1) * LANES
        q_ref[h, :, 0:LANES] = (q[:, lo:hi_] * scale).astype(BF16)
        q_ref[h, :, LANES:2 * LANES] = (_rope(q[:, hd + lo:hd + hi_], cos, sin) * scale).astype(BF16)
        k_ref[h, :, 0:LANES] = kv[:, lo:hi_].astype(BF16)
        k_ref[h, :, LANES:2 * LANES] = kpe
        v_ref[h] = kv[:, hd + lo:hd + hi_].astype(BF16)


def _mla_proj_call(lay, x, mod, g, cos_t, sin_t, wts, layer, j, heads, scale):
    wdq, qn, wuq, wdkv, kvn, wukv = wts
    tm, d = TM_PROJ, lay.d
    full = lambda arr: pl.BlockSpec((None,) + arr.shape[1:], lambda i: (j, 0, 0))
    return pl.pallas_call(
        functools.partial(_mla_proj_kernel, heads=heads, scale=scale),
        grid=(lay.n // tm,),
        in_specs=[pl.BlockSpec((tm, d), lambda i: (i, 0)),
                  pl.BlockSpec((None, 1, d), lambda i: (layer, 0, 0)),
                  lay.mod_spec(layer, 0, tm), lay.mod_spec(layer, 1, tm),
                  pl.BlockSpec((tm, LANES), lambda i: (i, 0)),
                  pl.BlockSpec((tm, LANES), lambda i: (i, 0)),
                  full(wdq), full(qn), full(wuq), full(wdkv), full(kvn), full(wukv)],
        out_specs=[pl.BlockSpec((heads, tm, 2 * LANES), lambda i: (0, i, 0)),
                   pl.BlockSpec((heads, tm, 2 * LANES), lambda i: (0, i, 0)),
                   pl.BlockSpec((heads, tm, LANES), lambda i: (0, i, 0))],
        out_shape=[jax.ShapeDtypeStruct((heads, lay.n, 2 * LANES), BF16),
                   jax.ShapeDtypeStruct((heads, lay.n, 2 * LANES), BF16),
                   jax.ShapeDtypeStruct((heads, lay.n, LANES), BF16)],
        compiler_params=_params(("parallel",)),
        name="mla_proj",
    )(x, g, mod, mod, cos_t, sin_t, wdq, qn, wuq, wdkv, kvn, wukv)


def _attn_kernel(*refs, tk, n_kv):
    q_ref, kc_ref, vc_ref = refs[:3]
    kl_ref, vl_ref = (refs[3], refs[4]) if n_kv else (None, None)
    o_ref = refs[-1]
    nt = (((1,), (1,)), ((), ()))
    hps = q_ref.shape[0]
    qs, ms, ls, accs = [], [], [], []
    for h in range(hps):
        q = q_ref[h]
        s = lax.dot_general(q, kc_ref[h], nt, preferred_element_type=F32)
        m = jnp.max(s, axis=-1, keepdims=True)
        p = jnp.exp2(s - m)
        qs.append(q)
        ms.append(m)
        ls.append(jnp.sum(p, axis=-1, keepdims=True))
        accs.append(jnp.dot(p.astype(BF16), vc_ref[h], preferred_element_type=F32))
    for i in range(n_kv):
        for h in range(hps):
            s = lax.dot_general(qs[h], kl_ref[h, i * tk:(i + 1) * tk, :], nt, preferred_element_type=F32)
            m_new = jnp.maximum(ms[h], jnp.max(s, axis=-1, keepdims=True))
            alpha = jnp.exp2(ms[h] - m_new)
            p = jnp.exp2(s - m_new)
            ls[h] = alpha * ls[h] + jnp.sum(p, axis=-1, keepdims=True)
            accs[h] = alpha * accs[h] + jnp.dot(p.astype(BF16), vl_ref[h, i * tk:(i + 1) * tk, :],
                                                preferred_element_type=F32)
            ms[h] = m_new
    for h in range(hps):
        o_ref[:, h * LANES:(h + 1) * LANES] = (accs[h] / ls[h]).astype(o_ref.dtype)


def _attn_call(lay, q, k, v, heads, need_ctx):
    nb, l, lc = lay.b, lay.l, lay.lc
    hps = ATTN_HEADS
    tq = min(TQ_ATTN, l)
    tk = min(TK_ATTN, l)
    nlq = l // tq
    ctx_blk0 = (nb * l) // lc
    ctx_specs = [pl.BlockSpec((hps, lc, 2 * LANES), lambda b, h, qi: (h, ctx_blk0 + b, 0)),
                 pl.BlockSpec((hps, lc, LANES), lambda b, h, qi: (h, ctx_blk0 + b, 0))]
    o_lat = pl.pallas_call(
        functools.partial(_attn_kernel, tk=tk, n_kv=l // tk),
        grid=(nb, heads // hps, nlq),
        in_specs=[pl.BlockSpec((hps, tq, 2 * LANES), lambda b, h, qi: (h, b * nlq + qi, 0))] + ctx_specs
                 + [pl.BlockSpec((hps, l, 2 * LANES), lambda b, h, qi: (h, b, 0)),
                    pl.BlockSpec((hps, l, LANES), lambda b, h, qi: (h, b, 0))],
        out_specs=pl.BlockSpec((tq, hps * LANES), lambda b, h, qi: (b * nlq + qi, h)),
        out_shape=jax.ShapeDtypeStruct((lay.n_lat, heads * LANES), BF16),
        compiler_params=_params(("parallel", "parallel", "arbitrary")),
        name="mla_attn",
    )(q, k, v, k, v)
    if not need_ctx:
        return o_lat, None
    o_ctx = pl.pallas_call(
        functools.partial(_attn_kernel, tk=tk, n_kv=0),
        grid=(nb, heads // hps, 1),
        in_specs=[pl.BlockSpec((hps, lc, 2 * LANES), lambda b, h, qi: (h, ctx_blk0 + b, 0))] + ctx_specs,
        out_specs=pl.BlockSpec((lc, hps * LANES), lambda b, h, qi: (b, h)),
        out_shape=jax.ShapeDtypeStruct((nb * lc, heads * LANES), BF16),
        compiler_params=_params(("parallel", "parallel", "arbitrary")),
        name="mla_attn_ctx",
    )(q, k, v)
    return o_lat, o_ctx


def _oproj_kernel(*refs, n_lat_tiles):
    ol_ref = refs[0]
    oc_ref = refs[1] if len(refs) == 6 else None
    x_ref, gate_ref, w_ref, out_ref = refs[-4:]
    o = ol_ref[...]
    if oc_ref is not None:
        o = jnp.where(pl.program_id(0) < n_lat_tiles, o, oc_ref[...])
    out_ref[...] = x_ref[...] + gate_ref[...] * jnp.dot(o, w_ref[...], preferred_element_type=F32)


def _oproj_call(lay, o_lat, o_ctx, x, mod, w, layer, j, n_rows):
    tm, d = TM_TOKEN, lay.d
    nlt = lay.n_lat // tm
    hd = o_lat.shape[1]
    o_specs = [pl.BlockSpec((tm, hd), lambda i: (jnp.minimum(i, nlt - 1), 0))]
    if o_ctx is not None:
        o_specs.append(pl.BlockSpec((tm, hd), lambda i: (jnp.maximum(i - nlt, 0), 0)))
    return pl.pallas_call(
        functools.partial(_oproj_kernel, n_lat_tiles=nlt),
        grid=(n_rows // tm,),
        in_specs=o_specs + [pl.BlockSpec((tm, d), lambda i: (i, 0)),
                            lay.mod_spec(layer, 2, tm),
                            pl.BlockSpec((None, hd, d), lambda i: (j, 0, 0))],
        out_specs=pl.BlockSpec((tm, d), lambda i: (i, 0)),
        out_shape=jax.ShapeDtypeStruct((n_rows, d), F32),
        compiler_params=_params(("parallel",)),
        name="mla_oproj",
    )(*([o_lat] + ([] if o_ctx is None else [o_ctx]) + [x, mod, w]))


def _final_kernel(x_ref, g_ref, o_ref):
    o_ref[...] = _rms(x_ref[...]) * g_ref[...]


def _final_call(lay, x, g):
    tm, d = TM_TOKEN, lay.d
    return pl.pallas_call(
        _final_kernel,
        grid=(lay.n_lat // tm,),
        in_specs=[pl.BlockSpec((tm, d), lambda i: (i, 0)),
                  pl.BlockSpec((1, d), lambda i: (0, 0))],
        out_specs=pl.BlockSpec((tm, d), lambda i: (i, 0)),
        out_shape=jax.ShapeDtypeStruct((lay.n_lat, d), F32),
        compiler_params=_params(("parallel",)),
        name="final_norm",
    )(x, g)


def _to_chunk_major(t):
    b, n, d = t.shape
    return jnp.transpose(t.reshape(b, n // CHUNK, CHUNK, d), (0, 2, 1, 3)).reshape(b * n, d)


def _from_chunk_major(t, b):
    n, d = t.shape[0] // b, t.shape[1]
    return jnp.transpose(t.reshape(b, CHUNK, n // CHUNK, d), (0, 2, 1, 3)).reshape(b, n, d)


def _pad_rope_cols(w):
    nf = w.shape[-1] // 4
    w4 = w.reshape(w.shape[:-1] + (2, 2, nf))
    z = jnp.zeros(w.shape[:-1] + (LANES // 2 - 2 * nf,), w.dtype)
    x1 = w4[..., 0, :].reshape(w.shape[:-1] + (2 * nf,))
    x2 = w4[..., 1, :].reshape(w.shape[:-1] + (2 * nf,))
    return jnp.concatenate([x1, z, x2, z], axis=-1)


def _rope_tables(lay, rope_dim):
    nf = rope_dim // 4
    pos = jnp.arange(lay.l)
    inv = ROPE_THETA ** (-jnp.arange(nf, dtype=F32) / nf)
    row = (pos // GRID_W).astype(F32)
    col = (pos % GRID_W).astype(F32)
    ang = jnp.concatenate([row[:, None] * inv, col[:, None] * inv], axis=1)
    z = jnp.zeros((lay.l, LANES // 2 - 2 * nf), F32)
    cos_l = jnp.concatenate([jnp.cos(ang), z, jnp.cos(ang), z], axis=1)
    sin_l = jnp.concatenate([-jnp.sin(ang), z, jnp.sin(ang), z], axis=1)
    cos_l = _to_chunk_major(jnp.broadcast_to(cos_l[None], (lay.b, lay.l, LANES)))
    sin_l = _to_chunk_major(jnp.broadcast_to(sin_l[None], (lay.b, lay.l, LANES)))
    n_ctx = lay.b * lay.lc
    cos_t = jnp.concatenate([cos_l, jnp.ones((n_ctx, LANES), F32)], axis=0)
    sin_t = jnp.concatenate([sin_l, jnp.zeros((n_ctx, LANES), F32)], axis=0)
    return cos_t, sin_t


def _mla_weights(w_dq, q_norm, w_uq, w_dkv, kv_norm, w_ukv, heads, nope, rope, vdim):
    n, ql = w_uq.shape[:2]
    kvl = kv_norm.shape[1]
    uq = w_uq.reshape(n, ql, heads, nope + rope)
    uq_pe = _pad_rope_cols(uq[..., nope:]).reshape(n, ql, heads * LANES)
    wuq = jnp.concatenate([uq[..., :nope].reshape(n, ql, heads * nope), uq_pe], axis=-1)
    wdkv = jnp.concatenate([w_dkv[..., :kvl], _pad_rope_cols(w_dkv[..., kvl:])], axis=-1)
    ukv = w_ukv.reshape(n, kvl, heads, nope + vdim)
    wukv = jnp.concatenate([ukv[..., :nope].reshape(n, kvl, heads * nope),
                            ukv[..., nope:].reshape(n, kvl, heads * vdim)], axis=-1)
    return (w_dq.astype(BF16), q_norm.reshape(n, 1, ql).astype(F32), wuq.astype(BF16),
            wdkv.astype(BF16), kv_norm.reshape(n, 1, kvl).astype(F32), wukv.astype(BF16))


def kernel(x, c, ctx, c_ctx, ada_w, ada_b, norm1_g, norm2_g, final_norm_g, s5_lam_re, s5_lam_im, s5_log_dt, s5_b_re, s5_b_im, s5_c_re, s5_c_im, s5_d, s5_w_glu, s5_b_glu, mla_w_dq, mla_q_norm, mla_w_uq, mla_w_dkv, mla_kv_norm, mla_w_ukv, mla_w_o, ffn_w1, ffn_w2):
    b, l, d = x.shape
    lc = ctx.shape[1]
    depth = ada_w.shape[0]
    lay = _Layout(b, l, lc, d)

    kvl = mla_kv_norm.shape[1]
    rope = mla_w_dkv.shape[2] - kvl
    heads = (mla_w_uq.shape[2] + mla_w_o.shape[1] - mla_w_ukv.shape[2]) // rope
    vdim = mla_w_o.shape[1] // heads
    nope = mla_w_uq.shape[2] // heads - rope
    assert nope == LANES and vdim == LANES and 4 * (rope // 4) == rope and rope <= LANES
    assert b + 1 <= SUBLANES and s5_b_re.shape[-1] == CHUNK
    scale = (nope + rope) ** -0.5 * math.log2(math.e)

    xs =jnp.concatenate([_to_chunk_major(x), _to_chunk_major(ctx)], axis=0)

    cc = jnp.concatenate([c, c_ctx[None]], axis=0)
    mod = _ada_table(cc, ada_w, ada_b).reshape(depth, SUBLANES, 1, 6 * d)
    g1 = norm1_g.reshape(depth, 1, d)
    g2 = norm2_g.reshape(depth, 1, d)
    cos_t, sin_t = _rope_tables(lay, rope)

    tabs = _s5_tables(s5_lam_re, s5_lam_im, s5_log_dt, s5_b_re, s5_b_im, s5_c_re, s5_c_im, s5_d)
    wts = _mla_weights(mla_w_dq, mla_q_norm, mla_w_uq, mla_w_dkv, mla_kv_norm, mla_w_ukv,
                       heads, nope, rope, vdim)
    w_glu, b_glu = s5_w_glu.astype(BF16), s5_b_glu.reshape(-1, 1, d)
    w_o = mla_w_o.astype(BF16)
    w1, w2 = ffn_w1.astype(BF16), ffn_w2.astype(BF16)

    for i in range(depth):
        j = i // 2
        n_rows = lay.n_lat if i == depth - 1 else lay.n
        if i % 2 == 0:
            a = _norm_call(lay, xs, mod, g1, i)
            y_l, y_c = _s5_core_call(lay, a, tabs, j)
            xs = _glu_call(lay, y_l, y_c, xs, mod, w_glu, b_glu, i, j)
        else:
            q, k, v = _mla_proj_call(lay, xs, mod, g1, cos_t, sin_t, wts, i, j, heads, scale)
            o_lat, o_ctx = _attn_call(lay, q, k, v, heads, n_rows == lay.n)
            xs = _oproj_call(lay, o_lat, o_ctx, xs, mod, w_o, i, j, n_rows)
        xs = _ffn_call(lay, xs, mod, g2, w1, w2, i, n_rows)

    out = _final_call(lay, xs, final_norm_g.reshape(1, d))
    return _from_chunk_major(out, b)
```

```python
import functools
import math

import jax
import jax.numpy as jnp
from jax import lax
from jax.experimental import pallas as pl
from jax.experimental.pallas import tpu as pltpu

F32 = jnp.float32
BF16 = jnp.bfloat16

EPS = 1e-6
GRID_W = 64
ROPE_THETA = 10000.0

LANES = 128
SUBLANES = 8
CHUNK = 16
VMEM_LIMIT = 56 * 1024 * 1024

TM_TOKEN = 512
TM_PROJ = 256
TQ_ATTN = 1024
ATTN_HEADS = 2
TK_ATTN = 256
TF_FFN = 1024
TN_ADA = 1024


def _params(sem):
    return pltpu.CompilerParams(dimension_semantics=sem, vmem_limit_bytes=VMEM_LIMIT)


def _rms(x):
    return x * lax.rsqrt(jnp.mean(x * x, axis=-1, keepdims=True) + EPS)


def _norm_mod(x, g, sc, sh):
    return (_rms(x) * g) * (1.0 + sc) + sh


def _gelu_tanh(y):
    c = math.sqrt(2.0 / math.pi)
    return 0.5 * y * (1.0 + jnp.tanh(c * (y + 0.044715 * (y * y * y))))


def _ada_kernel(cb_ref, w_ref, b_ref, o_ref, sb_s):
    n_rows, d, _ = cb_ref.shape
    n_lt = w_ref.shape[1] // LANES
    @pl.when(pl.program_id(1) == 0)
    def _():
        cb = cb_ref[...]
        sb_s[...] = cb * jax.nn.sigmoid(cb)
    rows = lax.broadcasted_iota(jnp.int32, (SUBLANES, 1), 0)

    def body(kb, accs):
        k0 = pl.multiple_of(kb * SUBLANES, SUBLANES)
        w = w_ref[pl.ds(k0, SUBLANES), :]
        return tuple(acc + w * jnp.concatenate([sb_s[r, pl.ds(k0, SUBLANES), :]] * n_lt, axis=1)
                     for r, acc in enumerate(accs))
    accs = lax.fori_loop(0, d // SUBLANES, body, (jnp.zeros(o_ref.shape, F32),) * n_rows, unroll=4)
    out = jnp.broadcast_to(b_ref[...], o_ref.shape)
    for r, acc in enumerate(accs):
        out = out + jnp.where(rows == r, jnp.sum(acc, axis=0, keepdims=True), 0.0)
    o_ref[...] = out


def _ada_table(cc, ada_w, ada_b):
    depth, d, n6 = ada_w.shape
    n_rows = cc.shape[0]
    tn = min(TN_ADA, n6)
    cb = jnp.broadcast_to(cc[:, :, None], (n_rows, d, LANES))
    return pl.pallas_call(
        _ada_kernel,
        grid=(depth, n6 // tn),
        in_specs=[pl.BlockSpec((n_rows, d, LANES), lambda l, n: (0, 0, 0)),
                  pl.BlockSpec((None, d, tn), lambda l, n: (l, 0, n)),
                  pl.BlockSpec((None, 1, tn), lambda l, n: (l, 0, n))],
        out_specs=pl.BlockSpec((None, SUBLANES, tn), lambda l, n: (l, 0, n)),
        out_shape=jax.ShapeDtypeStruct((depth, SUBLANES, n6), F32),
        scratch_shapes=[pltpu.VMEM((n_rows, d, LANES), F32)],
        compiler_params=_params(("parallel", "arbitrary")),
        name="ada_table",
    )(cb, ada_w, ada_b.reshape(depth, 1, n6))


class _Layout:
    def __init__(self, b, l, lc, d):
        self.b, self.l, self.lc, self.d = b, l, lc, d
        self.n_lat = b * l
        self.n = b * (l + lc)

    def mod_row(self, tm):
        n_lat_tiles = self.n_lat // tm
        per_batch = self.l // tm
        ctx_row = self.b

        def row(i):
            return jnp.where(i < n_lat_tiles, i // per_batch, ctx_row)
        return row

    def mod_spec(self, layer, chunk, tm, ngrid=1):
        row = self.mod_row(tm)
        if ngrid == 1:
            return pl.BlockSpec((None, None, 1, self.d), lambda i: (layer, row(i), 0, chunk))
        return pl.BlockSpec((None, None, 1, self.d), lambda i, f: (layer, row(i), 0, chunk))


def _norm_kernel(x_ref, g_ref, sh_ref, sc_ref, o_ref):
    o_ref[...] = _norm_mod(x_ref[...], g_ref[...], sc_ref[...], sh_ref[...]).astype(o_ref.dtype)


def _norm_call(lay, x, mod, g, layer):
    tm, d = TM_TOKEN, lay.d
    return pl.pallas_call(
        _norm_kernel,
        grid=(lay.n // tm,),
        in_specs=[pl.BlockSpec((tm, d), lambda i: (i, 0)),
                  pl.BlockSpec((None, 1, d), lambda i: (layer, 0, 0)),
                  lay.mod_spec(layer, 0, tm), lay.mod_spec(layer, 1, tm)],
        out_specs=pl.BlockSpec((tm, d), lambda i: (i, 0)),
        out_shape=jax.ShapeDtypeStruct((lay.n, d), BF16),
        compiler_params=_params(("parallel",)),
        name="norm_mod",
    )(x, g, mod, mod)


def _ffn_kernel(x_ref, g_ref, sh_ref, sc_ref, gate_ref, w1_ref, w2_ref, *rest):
    fg_ref = rest[0] if len(rest) == 4 else None
    o_ref, a_ref, acc_ref = rest[-3:]
    f = pl.program_id(1)

    @pl.when(f == 0)
    def _():
        a_ref[...] = _norm_mod(x_ref[...], g_ref[...], sc_ref[...], sh_ref[...]).astype(BF16)
        acc_ref[...] = jnp.zeros_like(acc_ref)

    h = jnp.dot(a_ref[...], w1_ref[...], preferred_element_type=F32)
    h = jnp.maximum(h, 0.0)
    h = (h * h).astype(BF16)
    acc_ref[...] += jnp.dot(h, w2_ref[...], preferred_element_type=F32)

    @pl.when(f == pl.num_programs(1) - 1)
    def _():
        out = x_ref[...] + gate_ref[...] * acc_ref[...]
        o_ref[...] = out if fg_ref is None else _rms(out) * fg_ref[...]


def _ffn_call(lay, x, mod, g, w1, w2, layer, wl, n_rows, final_g=None):
    tm, d = TM_TOKEN, lay.d
    extra_specs = [] if final_g is None else [pl.BlockSpec((1, d), lambda i, f: (0, 0))]
    extra = [] if final_g is None else [final_g]
    dff = w1.shape[2]
    tf = min(TF_FFN, dff)
    return pl.pallas_call(
        _ffn_kernel,
        grid=(n_rows // tm, dff // tf),
        in_specs=[pl.BlockSpec((tm, d), lambda i, f: (i, 0)),
                  pl.BlockSpec((None, 1, d), lambda i, f: (layer, 0, 0)),
                  lay.mod_spec(layer, 3, tm, 2), lay.mod_spec(layer, 4, tm, 2),
                  lay.mod_spec(layer, 5, tm, 2),
                  pl.BlockSpec((None, d, tf), lambda i, f: (wl, 0, f)),
                  pl.BlockSpec((None, tf, d), lambda i, f: (wl, f, 0))] + extra_specs,
        out_specs=pl.BlockSpec((tm, d), lambda i, f: (i, 0)),
        out_shape=jax.ShapeDtypeStruct((n_rows, d), F32),
        scratch_shapes=[pltpu.VMEM((tm, d), BF16), pltpu.VMEM((tm, d), F32)],
        compiler_params=_params(("parallel", "arbitrary")),
        name="ffn",
    )(x, g, mod, mod, mod, w1, w2, *extra)


def _cmul(ar, ai, br, bi):
    return ar * br - ai * bi, ar * bi + ai * br


def _tile_scan(sr, si, cr, ci, ap_ref, d, reverse):
    rows = lax.broadcasted_iota(jnp.int32, sr.shape, 0)

    def shift(v, k):
        if reverse:
            return jnp.where(rows < SUBLANES - k, pltpu.roll(v, SUBLANES - k, 0), 0.0)
        return jnp.where(rows >= k, pltpu.roll(v, k, 0), 0.0)

    pr, pi = sr, si
    for k in (1, 2, 4):
        akr = ap_ref[d, 0, k:k + 1, :]
        aki = ap_ref[d, 1, k:k + 1, :]
        tr, ti = _cmul(akr, aki, shift(pr, k), shift(pi, k))
        pr, pi = pr + tr, pi + ti
    base = SUBLANES if reverse else 0
    tab_r = ap_ref[d, 0, base:base + SUBLANES, :]
    tab_i = ap_ref[d, 1, base:base + SUBLANES, :]
    tr, ti = _cmul(tab_r, tab_i, cr, ci)
    hr, hi = shift(pr, 1) + tr, shift(pi, 1) + ti
    a8r = ap_ref[d, 0, 2 * SUBLANES:2 * SUBLANES + 1, :]
    a8i = ap_ref[d, 1, 2 * SUBLANES:2 * SUBLANES + 1, :]
    tr, ti = _cmul(a8r, a8i, cr, ci)
    last = 0 if reverse else SUBLANES - 1
    return hr, hi, pr[last:last + 1, :] + tr, pi[last:last + 1, :] + ti


def _split_bf16(v):
    hi = v.astype(BF16)
    return hi, (v - hi.astype(F32)).astype(BF16)


def _s5_core_kernel(*refs, nb, kl, kc):
    al_refs = refs[:nb]
    ac_refs = refs[nb:2 * nb]
    lamt_ref, bt_ref, laml_ref, ct_ref, ap_ref, dsk_ref, yl_ref, yc_ref = refs[2 * nb:2 * nb + 8]
    x_s, w_s, xc_s, wintra_s, bd_s, s_s, y_s = refs[2 * nb + 8:]
    gpl = LANES // CHUNK
    ns = x_s.shape[1] // LANES
    sw = s_s.shape[2]
    half = sw // 2
    lat_rows = nb * kl
    p_rows = ct_ref.shape[2]
    lane_group_p = lax.broadcasted_iota(jnp.int32, (p_rows, LANES), 1) // CHUNK

    for s in range(ns):
        for b in range(nb):
            x_s[b * kl:(b + 1) * kl, s * LANES:(s + 1) * LANES] = al_refs[b][s]
            x_s[lat_rows + b * kc:lat_rows + (b + 1) * kc, s * LANES:(s + 1) * LANES] = ac_refs[b][s]
    x = x_s[...]

    zero_tile = jnp.zeros((CHUNK, LANES), BF16)
    n_lt = half // LANES
    row_i = lax.broadcasted_iota(jnp.int32, (LANES, LANES), 0)
    lane_i = lax.broadcasted_iota(jnp.int32, (LANES, LANES), 1)
    same_group = row_i // CHUNK == lane_i // CHUNK
    even_lo = ((row_i // CHUNK) % 2 == 0) & (lane_i < LANES // 2)
    odd_hi = ((row_i // CHUNK) % 2 == 1) & (lane_i >= LANES // 2)
    k0 = jnp.where(row_i == lane_i, dsk_ref[...], 0.0)
    for d in range(2):
        cur, swapped = bt_ref[d, 0], bt_ref[d, 1]
        lam_a, lam_b = lamt_ref[d, 0], lamt_ref[d, 1]
        for e in range(ns):
            s = ns - 1 - e if d == 0 else e
            hi, lo = _split_bf16(cur)
            xc_s[0, s * LANES:(s + 1) * LANES, :] = hi
            xc_s[1, s * LANES:(s + 1) * LANES, :] = lo
            re_pos = jnp.where(even_lo, cur, jnp.where(odd_hi, swapped, 0.0)).astype(BF16)
            im_pos = jnp.where(even_lo, swapped, jnp.where(odd_hi, cur, 0.0)).astype(BF16)
            for g in range(gpl):
                r0 = g * CHUNK
                tiles = ([re_pos[r0:r0 + CHUNK, :] if lt == g // 2 else zero_tile for lt in range(n_lt)]
                         + [im_pos[r0:r0 + CHUNK, :] if lt == g // 2 else zero_tile for lt in range(n_lt)])
                w_s[d, s * LANES + r0:s * LANES + r0 + CHUNK, :] = jnp.concatenate(tiles, axis=-1)
            cur, swapped = lam_a * cur + lam_b * swapped, lam_a * swapped - lam_b * cur
        s_s[d] = jnp.dot(x, w_s[d], preferred_element_type=F32)
        c_hi, c_lo = _split_bf16(jnp.concatenate([ct_ref[d, 0], -ct_ref[d, 1]], axis=0))
        kall = (jnp.dot(xc_s[0], c_hi, preferred_element_type=F32)
                + jnp.dot(xc_s[0], c_lo, preferred_element_type=F32)
                + jnp.dot(xc_s[1], c_hi, preferred_element_type=F32))
        for e in range(ns):
            s = ns - 1 - e if d == 0 else e
            ktile = jnp.where(same_group, kall[s * LANES:(s + 1) * LANES, :], 0.0)
            if e == 0:
                k0 = k0 + ktile
            else:
                bd_s[ns - 1 + (e if d == 0 else -e)] = ktile.astype(BF16)
    bd_s[ns - 1] = k0.astype(BF16)

    n_ctx_tiles = kc // SUBLANES
    n_lat_tiles = kl // SUBLANES

    for s in range(ns):
        for t in range(ns):
            wintra_s[s * LANES:(s + 1) * LANES, t * LANES:(t + 1) * LANES] = bd_s[t - s + ns - 1]
    y_s[...] = jnp.dot(x, wintra_s[...], preferred_element_type=F32)

    carry = [jnp.zeros((1, half), F32)] * (4 * nb)
    for i in range(n_ctx_tiles + n_lat_tiles):
        for d in range(2):
            for b in range(nb):
                cr, ci = carry[2 * (d * nb + b)], carry[2 * (d * nb + b) + 1]
                if i < n_ctx_tiles:
                    row = lat_rows + b * kc + (i if d == 0 else n_ctx_tiles - 1 - i) * SUBLANES
                else:
                    il = i - n_ctx_tiles
                    row = b * kl + (il if d == 0 else n_lat_tiles - 1 - il) * SUBLANES
                sr = s_s[d, row:row + SUBLANES, 0:half]
                si = s_s[d, row:row + SUBLANES, half:sw]
                hr, hi, cr, ci = _tile_scan(sr, si, cr, ci, ap_ref, d, d == 1)
                s_s[d, row:row + SUBLANES, 0:half] = hr
                s_s[d, row:row + SUBLANES, half:sw] = hi
                carry[2 * (d * nb + b)], carry[2 * (d * nb + b) + 1] = cr, ci

    hcols = (ns * LANES) // 2
    for d in range(2):
        lam_r, lam_i = laml_ref[d, 0], laml_ref[d, 1]
        cur_r, cur_i = _cmul(lam_r, lam_i, ct_ref[d, 0], ct_ref[d, 1])
        for e in range(ns):
            t = e if d == 0 else ns - 1 - e
            c0 = t * LANES
            rbase = (c0 // hcols) * sw
            cc = c0 % hcols
            for part, ctile in ((0, cur_r), (1, -cur_i)):
                for g in range(gpl):
                    r0 = rbase + part * half + g * p_rows
                    w_s[d, r0:r0 + p_rows, cc:cc + LANES] = jnp.where(
                        lane_group_p == g, ctile, 0.0).astype(BF16)
            cur_r, cur_i = _cmul(lam_r, lam_i, cur_r, cur_i)
    for d in range(2):
        h = s_s[d].astype(BF16)
        for c in range(2):
            y_s[:, c * hcols:(c + 1) * hcols] += jnp.dot(
                h, w_s[d, c * sw:(c + 1) * sw, :], preferred_element_type=F32)

    for t in range(ns):
        for b in range(nb):
            yl_ref[b * ns + t] = y_s[b * kl:(b + 1) * kl, t * LANES:(t + 1) * LANES]
            yc_ref[b * ns + t] = y_s[lat_rows + b * kc:lat_rows + (b + 1) * kc, t * LANES:(t + 1) * LANES]


def _s5_core_call(lay, a, tabs, layer):
    lamt, bt, laml, ct, apow, dsk = tabs
    nb, d = lay.b, lay.d
    kl, kc = lay.l // CHUNK, lay.lc // CHUNK
    nj = d // LANES
    rows = nb * (kl + kc)
    sw = 2 * (LANES // CHUNK) * ct.shape[-2]
    assert 2 * sw == CHUNK * LANES
    a_l = a.reshape(lay.n // kl, kl, d)
    a_c = a.reshape(lay.n // kc, kc, d)
    ctx_blk0 = (nb * kl) // kc

    def tab_spec(t):
        return pl.BlockSpec((None, 2, t.shape[2], None) + t.shape[4:], lambda j: (layer, 0, 0, j, 0, 0))
    in_specs = ([pl.BlockSpec((CHUNK, kl, LANES), functools.partial(lambda j, b: (b, 0, j), b=b))
                 for b in range(nb)]
                + [pl.BlockSpec((CHUNK, kc, LANES), functools.partial(lambda j, b: (ctx_blk0 + b, 0, j), b=b))
                   for b in range(nb)]
                + [tab_spec(lamt), tab_spec(bt), tab_spec(laml), tab_spec(ct), tab_spec(apow),
                   pl.BlockSpec((None, None, 1, LANES), lambda j: (layer, j, 0, 0))])
    y_l, y_c = pl.pallas_call(
        functools.partial(_s5_core_kernel, nb=nb, kl=kl, kc=kc),
        grid=(nj,),
        in_specs=in_specs,
        out_specs=[pl.BlockSpec((nb * CHUNK, kl, LANES), lambda j: (0, 0, j)),
                   pl.BlockSpec((nb * CHUNK, kc, LANES), lambda j: (0, 0, j))],
        out_shape=[jax.ShapeDtypeStruct((nb * CHUNK, kl, d), F32),
                   jax.ShapeDtypeStruct((nb * CHUNK, kc, d), F32)],
        scratch_shapes=[pltpu.VMEM((rows, CHUNK * LANES), BF16),
                        pltpu.VMEM((2, CHUNK * LANES, sw), BF16),
                        pltpu.VMEM((2, CHUNK * LANES, LANES), BF16),
                        pltpu.VMEM((CHUNK * LANES, CHUNK * LANES), BF16),
                        pltpu.VMEM((2 * CHUNK - 1, LANES, LANES), BF16),
                        pltpu.VMEM((2, rows, sw), F32),
                        pltpu.VMEM((rows, CHUNK * LANES), F32)],
        compiler_params=_params(("parallel",)),
        name="s5_core",
    )(*([a_l] * nb + [a_c] * nb + [lamt, bt, laml, ct, apow, dsk]))
    return y_l.reshape(nb * lay.l, d), y_c.reshape(nb * lay.lc, d)


def _s5_tables(lam_re, lam_im, log_dt, b_re, b_im, c_re, c_im, d_skip):
    nl, _, ng, p = lam_re.shape
    grp = b_re.shape[-1]
    gpl = LANES // grp
    nj = ng // gpl
    dt = jnp.exp(log_dt.astype(F32))[..., None]
    lr, li = lam_re.astype(F32), lam_im.astype(F32)

    mag = jnp.exp(lr * dt)
    lbr, lbi = mag * jnp.cos(li * dt), mag * jnp.sin(li * dt)
    nr = lbr - 1.0
    den = lr * lr + li * li
    fr = (nr * lr + lbi * li) / den
    fi = (lbi * lr - nr * li) / den
    bbr = fr[..., None] * b_re - fi[..., None] * b_im
    bbi = fr[..., None] * b_im + fi[..., None] * b_re

    def rows_gc(re, im):
        v = jnp.concatenate([re, im], axis=-1)[:, :, :, None, :]
        return jnp.broadcast_to(v, (nl, 2, ng, grp, 2 * p)).reshape(nl, 2, nj, gpl * grp, 2 * p)

    def rows_p(v):
        v = v.reshape(nl, 2, nj, gpl, grp, p)
        return jnp.transpose(v, (0, 1, 2, 5, 3, 4)).reshape(nl, 2, nj, p, gpl * grp)

    lamt = jnp.stack([rows_gc(lbr, lbr), rows_gc(-lbi, lbi)], axis=2)
    bbr_t, bbi_t = jnp.swapaxes(bbr, -1, -2), jnp.swapaxes(bbi, -1, -2)
    bt = jnp.stack([jnp.concatenate([bbr_t, bbi_t], axis=-1), jnp.concatenate([bbi_t, bbr_t], axis=-1)], axis=2)
    bt = bt.reshape(nl, 2, 2, nj, gpl * grp, 2 * p)
    lam_bc = lambda v: jnp.broadcast_to(v[:, :, :, None, :], (nl, 2, ng, grp, p))
    laml = jnp.stack([rows_p(lam_bc(lbr)), rows_p(lam_bc(lbi))], axis=2)
    ct = jnp.stack([rows_p(c_re.astype(F32)), rows_p(c_im.astype(F32))], axis=2)

    m = (CHUNK * jnp.arange(SUBLANES + 1, dtype=F32))[:, None, None, None, None]
    amag = jnp.exp(lr * dt * m)

    def lay_a(v):
        tab = jnp.concatenate([v[:SUBLANES], v[:SUBLANES][::-1], v[SUBLANES:],
                               jnp.zeros((SUBLANES - 1,) + v.shape[1:], F32)], axis=0)
        tab = tab.reshape(3 * SUBLANES, nl, 2, nj, gpl * p)
        return jnp.transpose(tab, (1, 2, 3, 0, 4))
    apow = jnp.stack([lay_a(amag * jnp.cos(li * dt * m)), lay_a(amag * jnp.sin(li * dt * m))], axis=2)
    dsk = d_skip.astype(F32).reshape(nl, nj, 1, LANES)
    return lamt, bt, laml, ct, apow, dsk


def _glu_kernel(yl_ref, yc_ref, x_ref, gate_ref, w_ref, b_ref, o_ref, *, n_lat_tiles):
    i = pl.program_id(0)
    y = jnp.where(i < n_lat_tiles, yl_ref[...], yc_ref[...])
    z = _gelu_tanh(y)
    u = jnp.dot(z.astype(BF16), w_ref[...], preferred_element_type=F32) + b_ref[...]
    o_ref[...] = x_ref[...] + gate_ref[...] * (z * jax.nn.sigmoid(u))


def _glu_call(lay, y_l, y_c, x, mod, w, bias, layer, j):
    tm, d = TM_TOKEN, lay.d
    nlt = lay.n_lat // tm
    return pl.pallas_call(
        functools.partial(_glu_kernel, n_lat_tiles=nlt),
        grid=(lay.n // tm,),
        in_specs=[pl.BlockSpec((tm, d), lambda i: (jnp.minimum(i, nlt - 1), 0)),
                  pl.BlockSpec((tm, d), lambda i: (jnp.maximum(i - nlt, 0), 0)),
                  pl.BlockSpec((tm, d), lambda i: (i, 0)),
                  lay.mod_spec(layer, 2, tm),
                  pl.BlockSpec((None, d, d), lambda i: (j, 0, 0)),
                  pl.BlockSpec((None, 1, d), lambda i: (j, 0, 0))],
        out_specs=pl.BlockSpec((tm, d), lambda i: (i, 0)),
        out_shape=jax.ShapeDtypeStruct((lay.n, d), F32),
        compiler_params=_params(("parallel",)),
        name="s5_glu",
    )(y_l, y_c, x, mod, w, bias)


def _rope(v, cos, sin):
    return v * cos + pltpu.roll(v, LANES // 2, 1) * sin


def _mla_proj_kernel(x_ref, g_ref, sh_ref, sc_ref, cos_ref, sin_ref, wdq_ref, qn_ref, wuq_ref,
                     wdkv_ref, kvn_ref, wukv_ref, q_ref, k_ref, v_ref, *, heads, scale):
    a = _norm_mod(x_ref[...], g_ref[...], sc_ref[...], sh_ref[...]).astype(BF16)
    cos, sin = cos_ref[...], sin_ref[...]
    hd = heads * LANES
    cq = jnp.dot(a, wdq_ref[...], preferred_element_type=F32)
    cq = (_rms(cq) * qn_ref[...]).astype(BF16)
    q = jnp.dot(cq, wuq_ref[...], preferred_element_type=F32)
    ckv = jnp.dot(a, wdkv_ref[...], preferred_element_type=F32)
    kvl = ckv.shape[1] - LANES
    ckvn = (_rms(ckv[:, :kvl]) * kvn_ref[...]).astype(BF16)
    kpe = _rope(ckv[:, kvl:], cos, sin).astype(BF16)
    kv = jnp.dot(ckvn, wukv_ref[...], preferred_element_type=F32)
    for h in range(heads):
        lo, hi_ = h * LANES, (h + 1) * LANES
        q_ref[h, :, 0:LANES] = (q[:, lo:hi_] * scale).astype(BF16)
        q_ref[h, :, LANES:2 * LANES] = (_rope(q[:, hd + lo:hd + hi_], cos, sin) * scale).astype(BF16)
        k_ref[h, :, 0:LANES] = kv[:, lo:hi_].astype(BF16)
        k_ref[h, :, LANES:2 * LANES] = kpe
        v_ref[h] = kv[:, hd + lo:hd + hi_].astype(BF16)


def _mla_proj_call(lay, x, mod, g, cos_t, sin_t, wts, layer, j, heads, scale):
    wdq, qn, wuq, wdkv, kvn, wukv = wts
    tm, d = TM_PROJ, lay.d
    full = lambda arr: pl.BlockSpec((None,) + arr.shape[1:], lambda i: (j, 0, 0))
    return pl.pallas_call(
        functools.partial(_mla_proj_kernel, heads=heads, scale=scale),
        grid=(lay.n // tm,),
        in_specs=[pl.BlockSpec((tm, d), lambda i: (i, 0)),
                  pl.BlockSpec((None, 1, d), lambda i: (layer, 0, 0)),
                  lay.mod_spec(layer, 0, tm), lay.mod_spec(layer, 1, tm),
                  pl.BlockSpec((tm, LANES), lambda i: (i, 0)),
                  pl.BlockSpec((tm, LANES), lambda i: (i, 0)),
                  full(wdq), full(qn), full(wuq), full(wdkv), full(kvn), full(wukv)],
        out_specs=[pl.BlockSpec((heads, tm, 2 * LANES), lambda i: (0, i, 0)),
                   pl.BlockSpec((heads, tm, 2 * LANES), lambda i: (0, i, 0)),
                   pl.BlockSpec((heads, tm, LANES), lambda i: (0, i, 0))],
        out_shape=[jax.ShapeDtypeStruct((heads, lay.n, 2 * LANES), BF16),
                   jax.ShapeDtypeStruct((heads, lay.n, 2 * LANES), BF16),
                   jax.ShapeDtypeStruct((heads, lay.n, LANES), BF16)],
        compiler_params=_params(("parallel",)),
        name="mla_proj",
    )(x, g, mod, mod, cos_t, sin_t, wdq, qn, wuq, wdkv, kvn, wukv)


def _attn_kernel(*refs, tk, n_kv, n_cast=0):
    q_ref, kc_ref, vc_ref = refs[:3]
    kl_ref, vl_ref = (refs[3], refs[4]) if n_kv else (None, None)
    if n_cast:
        for src, dst in zip(refs[5:5 + n_cast], refs[-n_cast:]):
            dst[...] = src[...].astype(dst.dtype)
    o_ref = refs[-1 - n_cast]
    nt = (((1,), (1,)), ((), ()))
    hps = q_ref.shape[0]
    qs, ms, ls, accs = [], [], [], []
    for h in range(hps):
        q = q_ref[h]
        s = lax.dot_general(q, kc_ref[h], nt, preferred_element_type=F32)
        m = jnp.max(s, axis=-1, keepdims=True)
        p = jnp.exp2(s - m)
        qs.append(q)
        ms.append(m)
        ls.append(jnp.sum(p, axis=-1, keepdims=True))
        accs.append(jnp.dot(p.astype(BF16), vc_ref[h], preferred_element_type=F32))
    for i in range(n_kv):
        for h in range(hps):
            s = lax.dot_general(qs[h], kl_ref[h, i * tk:(i + 1) * tk, :], nt, preferred_element_type=F32)
            m_new = jnp.maximum(ms[h], jnp.max(s, axis=-1, keepdims=True))
            alpha = jnp.exp2(ms[h] - m_new)
            p = jnp.exp2(s - m_new)
            ls[h] = alpha * ls[h] + jnp.sum(p, axis=-1, keepdims=True)
            accs[h] = alpha * accs[h] + jnp.dot(p.astype(BF16), vl_ref[h, i * tk:(i + 1) * tk, :],
                                                preferred_element_type=F32)
            ms[h] = m_new
    for h in range(hps):
        o_ref[:, h * LANES:(h + 1) * LANES] = (accs[h] / ls[h]).astype(o_ref.dtype)


def _attn_call(lay, q, k, v, heads, need_ctx, cast):
    nb, l, lc = lay.b, lay.l, lay.lc
    hps = ATTN_HEADS
    tq = min(TQ_ATTN, l)
    tk = min(TK_ATTN, l)
    nlq = l // tq
    nh = heads // hps
    ctx_blk0 = (nb * l) // lc
    ctx_specs = [pl.BlockSpec((hps, lc, 2 * LANES), lambda b, h, qi: (h, ctx_blk0 + b, 0)),
                 pl.BlockSpec((hps, lc, LANES), lambda b, h, qi: (h, ctx_blk0 + b, 0))]
    arrays, first, count = cast
    steps = nb * nh * nlq
    cast_in, cast_specs, cast_out_specs, cast_shapes = [], [], [], []
    for w in arrays:
        rows, cols = w.shape[1:]
        blk = count * rows // steps
        assert blk * steps == count * rows and blk % CHUNK == 0
        blk0 = first * rows // blk
        cast_in.append(w.reshape(-1, cols))
        cast_specs.append(pl.BlockSpec(
            (blk, cols), functools.partial(lambda b, h, qi, o: (o + (b * nh + h) * nlq + qi, 0), o=blk0)))
        cast_out_specs.append(pl.BlockSpec((blk, cols), lambda b, h, qi: ((b * nh + h) * nlq + qi, 0)))
        cast_shapes.append(jax.ShapeDtypeStruct((count * rows, cols), BF16))
    outs = pl.pallas_call(
        functools.partial(_attn_kernel, tk=tk, n_kv=l // tk, n_cast=len(arrays)),
        grid=(nb, nh, nlq),
        in_specs=[pl.BlockSpec((hps, tq, 2 * LANES), lambda b, h, qi: (h, b * nlq + qi, 0))] + ctx_specs
                 + [pl.BlockSpec((hps, l, 2 * LANES), lambda b, h, qi: (h, b, 0)),
                    pl.BlockSpec((hps, l, LANES), lambda b, h, qi: (h, b, 0))] + cast_specs,
        out_specs=[pl.BlockSpec((tq, hps * LANES), lambda b, h, qi: (b * nlq + qi, h))] + cast_out_specs,
        out_shape=[jax.ShapeDtypeStruct((lay.n_lat, heads * LANES), BF16)] + cast_shapes,
        compiler_params=_params(("parallel", "parallel", "arbitrary")),
        name="mla_attn",
    )(q, k, v, k, v, *cast_in)
    o_lat = outs[0]
    converted = [o.reshape((count,) + w.shape[1:]) for o, w in zip(outs[1:], arrays)]
    if not need_ctx:
        return o_lat, None, converted
    o_ctx = pl.pallas_call(
        functools.partial(_attn_kernel, tk=tk, n_kv=0),
        grid=(nb, heads // hps, 1),
        in_specs=[pl.BlockSpec((hps, lc, 2 * LANES), lambda b, h, qi: (h, ctx_blk0 + b, 0))] + ctx_specs,
        out_specs=pl.BlockSpec((lc, hps * LANES), lambda b, h, qi: (b, h)),
        out_shape=jax.ShapeDtypeStruct((nb * lc, heads * LANES), BF16),
        compiler_params=_params(("parallel", "parallel", "arbitrary")),
        name="mla_attn_ctx",
    )(q, k, v)
    return o_lat, o_ctx, converted


def _oproj_kernel(*refs, n_lat_tiles):
    ol_ref = refs[0]
    oc_ref = refs[1] if len(refs) == 6 else None
    x_ref, gate_ref, w_ref, out_ref = refs[-4:]
    o = ol_ref[...]
    if oc_ref is not None:
        o = jnp.where(pl.program_id(0) < n_lat_tiles, o, oc_ref[...])
    out_ref[...] = x_ref[...] + gate_ref[...] * jnp.dot(o, w_ref[...], preferred_element_type=F32)


def _oproj_call(lay, o_lat, o_ctx, x, mod, w, layer, j, n_rows):
    tm, d = TM_TOKEN, lay.d
    nlt = lay.n_lat // tm
    hd = o_lat.shape[1]
    o_specs = [pl.BlockSpec((tm, hd), lambda i: (jnp.minimum(i, nlt - 1), 0))]
    if o_ctx is not None:
        o_specs.append(pl.BlockSpec((tm, hd), lambda i: (jnp.maximum(i - nlt, 0), 0)))
    return pl.pallas_call(
        functools.partial(_oproj_kernel, n_lat_tiles=nlt),
        grid=(n_rows // tm,),
        in_specs=o_specs + [pl.BlockSpec((tm, d), lambda i: (i, 0)),
                            lay.mod_spec(layer, 2, tm),
                            pl.BlockSpec((None, hd, d), lambda i: (j, 0, 0))],
        out_specs=pl.BlockSpec((tm, d), lambda i: (i, 0)),
        out_shape=jax.ShapeDtypeStruct((n_rows, d), F32),
        compiler_params=_params(("parallel",)),
        name="mla_oproj",
    )(*([o_lat] + ([] if o_ctx is None else [o_ctx]) + [x, mod, w]))


def _to_chunk_major(t):
    b, n, d = t.shape
    return jnp.transpose(t.reshape(b, n // CHUNK, CHUNK, d), (0, 2, 1, 3)).reshape(b * n, d)


def _from_chunk_major(t, b):
    n, d = t.shape[0] // b, t.shape[1]
    return jnp.transpose(t.reshape(b, CHUNK, n // CHUNK, d), (0, 2, 1, 3)).reshape(b, n, d)


def _pad_rope_cols(w):
    nf = w.shape[-1] // 4
    w4 = w.reshape(w.shape[:-1] + (2, 2, nf))
    z = jnp.zeros(w.shape[:-1] + (LANES // 2 - 2 * nf,), w.dtype)
    x1 = w4[..., 0, :].reshape(w.shape[:-1] + (2 * nf,))
    x2 = w4[..., 1, :].reshape(w.shape[:-1] + (2 * nf,))
    return jnp.concatenate([x1, z, x2, z], axis=-1)


def _rope_tables(lay, rope_dim):
    nf = rope_dim // 4
    pos = jnp.arange(lay.l)
    inv = ROPE_THETA ** (-jnp.arange(nf, dtype=F32) / nf)
    row = (pos // GRID_W).astype(F32)
    col = (pos % GRID_W).astype(F32)
    ang = jnp.concatenate([row[:, None] * inv, col[:, None] * inv], axis=1)
    z = jnp.zeros((lay.l, LANES // 2 - 2 * nf), F32)
    cos_l = jnp.concatenate([jnp.cos(ang), z, jnp.cos(ang), z], axis=1)
    sin_l = jnp.concatenate([-jnp.sin(ang), z, jnp.sin(ang), z], axis=1)
    cos_l = _to_chunk_major(jnp.broadcast_to(cos_l[None], (lay.b, lay.l, LANES)))
    sin_l = _to_chunk_major(jnp.broadcast_to(sin_l[None], (lay.b, lay.l, LANES)))
    n_ctx = lay.b * lay.lc
    cos_t = jnp.concatenate([cos_l, jnp.ones((n_ctx, LANES), F32)], axis=0)
    sin_t = jnp.concatenate([sin_l, jnp.zeros((n_ctx, LANES), F32)], axis=0)
    return cos_t, sin_t


def _mla_weights(w_dq, q_norm, w_uq, w_dkv, kv_norm, w_ukv, heads, nope, rope, vdim):
    n, ql = w_uq.shape[:2]
    kvl = kv_norm.shape[1]
    uq = w_uq.reshape(n, ql, heads, nope + rope)
    uq_pe = _pad_rope_cols(uq[..., nope:]).reshape(n, ql, heads * LANES)
    wuq = jnp.concatenate([uq[..., :nope].reshape(n, ql, heads * nope), uq_pe], axis=-1)
    wdkv = jnp.concatenate([w_dkv[..., :kvl], _pad_rope_cols(w_dkv[..., kvl:])], axis=-1)
    ukv = w_ukv.reshape(n, kvl, heads, nope + vdim)
    wukv = jnp.concatenate([ukv[..., :nope].reshape(n, kvl, heads * nope),
                            ukv[..., nope:].reshape(n, kvl, heads * vdim)], axis=-1)
    return (w_dq.astype(BF16), q_norm.reshape(n, 1, ql).astype(F32), wuq.astype(BF16),
            wdkv.astype(BF16), kv_norm.reshape(n, 1, kvl).astype(F32), wukv.astype(BF16))


def kernel(x, c, ctx, c_ctx, ada_w, ada_b, norm1_g, norm2_g, final_norm_g, s5_lam_re, s5_lam_im, s5_log_dt, s5_b_re, s5_b_im, s5_c_re, s5_c_im, s5_d, s5_w_glu, s5_b_glu, mla_w_dq, mla_q_norm, mla_w_uq, mla_w_dkv, mla_kv_norm, mla_w_ukv, mla_w_o, ffn_w1, ffn_w2):
    b, l, d = x.shape
    lc = ctx.shape[1]
    depth = ada_w.shape[0]
    lay = _Layout(b, l, lc, d)

    kvl = mla_kv_norm.shape[1]
    rope = mla_w_dkv.shape[2] - kvl
    heads = (mla_w_uq.shape[2] + mla_w_o.shape[1] - mla_w_ukv.shape[2]) // rope
    vdim = mla_w_o.shape[1] // heads
    nope = mla_w_uq.shape[2] // heads - rope
    assert nope == LANES and vdim == LANES and 4 * (rope // 4) == rope and rope <= LANES
    assert b + 1 <= SUBLANES and s5_b_re.shape[-1] == CHUNK
    scale = (nope + rope) ** -0.5 * math.log2(math.e)

    xs =jnp.concatenate([_to_chunk_major(x), _to_chunk_major(ctx)], axis=0)

    cc = jnp.concatenate([c, c_ctx[None]], axis=0)
    mod = _ada_table(cc, ada_w, ada_b).reshape(depth, SUBLANES, 1, 6 * d)
    g1 = norm1_g.reshape(depth, 1, d)
    g2 = norm2_g.reshape(depth, 1, d)
    cos_t, sin_t = _rope_tables(lay, rope)

    tabs = _s5_tables(s5_lam_re, s5_lam_im, s5_log_dt, s5_b_re, s5_b_im, s5_c_re, s5_c_im, s5_d)
    wts = _mla_weights(mla_w_dq, mla_q_norm, mla_w_uq, mla_w_dkv, mla_kv_norm, mla_w_ukv,
                       heads, nope, rope, vdim)
    w_glu, b_glu = s5_w_glu.astype(BF16), s5_b_glu.reshape(-1, 1, d)
    w_o = mla_w_o.astype(BF16)
    mlp_w = {0: (ffn_w1[:1].astype(BF16), ffn_w2[:1].astype(BF16), 0)}

    for i in range(depth):
        j = i // 2
        n_rows = lay.n_lat if i == depth - 1 else lay.n
        if i % 2 == 0:
            a = _norm_call(lay, xs, mod, g1, i)
            y_l, y_c = _s5_core_call(lay, a, tabs, j)
            xs = _glu_call(lay, y_l, y_c, xs, mod, w_glu, b_glu, i, j)
        else:
            q, k, v = _mla_proj_call(lay, xs, mod, g1, cos_t, sin_t, wts, i, j, heads, scale)
            count = min(2, depth - i)
            o_lat, o_ctx, (w1c, w2c) = _attn_call(lay, q, k, v, heads, n_rows == lay.n,
                                                  ([ffn_w1, ffn_w2], i, count))
            for t in range(count):
                mlp_w[i + t] = (w1c, w2c, t)
            xs = _oproj_call(lay, o_lat, o_ctx, xs, mod, w_o, i, j, n_rows)
        w1, w2, wl = mlp_w[i]
        final_g = final_norm_g.reshape(1, d) if i == depth - 1 else None
        xs = _ffn_call(lay, xs, mod, g2, w1, w2, i, wl, n_rows, final_g)

    return _from_chunk_major(xs, b)
```

```python
import functools
import math

import jax
import jax.numpy as jnp
from jax import lax
from jax.experimental import pallas as pl
from jax.experimental.pallas import tpu as pltpu

F32 = jnp.float32
BF16 = jnp.bfloat16

EPS = 1e-6
GRID_W = 64
ROPE_THETA = 10000.0

LANES = 128
SUBLANES = 8
CHUNK = 16
VMEM_LIMIT = 56 * 1024 * 1024

TM_TOKEN = 512
TM_PROJ = 512
TQ_ATTN = 1024
ATTN_HEADS = 2
TK_ATTN = 256
TF_FFN = 1024
TN_ADA = 1024
CAST_BLOCK_ELEMS = 2 * 1024 * 1024


def _params(sem):
    return pltpu.CompilerParams(dimension_semantics=sem, vmem_limit_bytes=VMEM_LIMIT)


def _rms(x):
    return x * lax.rsqrt(jnp.mean(x * x, axis=-1, keepdims=True) + EPS)


def _norm_mod(x, g, sc, sh):
    return (_rms(x) * g) * (1.0 + sc) + sh


def _gelu_tanh(y):
    c = math.sqrt(2.0 / math.pi)
    return 0.5 * y * (1.0 + jnp.tanh(c * (y + 0.044715 * (y * y * y))))


def _ada_kernel(cb_ref, w_ref, b_ref, o_ref, sb_s):
    n_rows, d, _ = cb_ref.shape
    n_lt = w_ref.shape[1] // LANES
    @pl.when(pl.program_id(1) == 0)
    def _():
        cb = cb_ref[...]
        sb_s[...] = cb * jax.nn.sigmoid(cb)
    rows = lax.broadcasted_iota(jnp.int32, (SUBLANES, 1), 0)

    def body(kb, accs):
        k0 = pl.multiple_of(kb * SUBLANES, SUBLANES)
        w = w_ref[pl.ds(k0, SUBLANES), :]
        return tuple(acc + w * jnp.concatenate([sb_s[r, pl.ds(k0, SUBLANES), :]] * n_lt, axis=1)
                     for r, acc in enumerate(accs))
    accs = lax.fori_loop(0, d // SUBLANES, body, (jnp.zeros(o_ref.shape, F32),) * n_rows, unroll=4)
    out = jnp.broadcast_to(b_ref[...], o_ref.shape)
    for r, acc in enumerate(accs):
        out = out + jnp.where(rows == r, jnp.sum(acc, axis=0, keepdims=True), 0.0)
    o_ref[...] = out


def _ada_table(cc, ada_w, ada_b):
    depth, d, n6 = ada_w.shape
    n_rows = cc.shape[0]
    tn = min(TN_ADA, n6)
    cb = jnp.broadcast_to(cc[:, :, None], (n_rows, d, LANES))
    return pl.pallas_call(
        _ada_kernel,
        grid=(depth, n6 // tn),
        in_specs=[pl.BlockSpec((n_rows, d, LANES), lambda l, n: (0, 0, 0)),
                  pl.BlockSpec((None, d, tn), lambda l, n: (l, 0, n)),
                  pl.BlockSpec((None, 1, tn), lambda l, n: (l, 0, n))],
        out_specs=pl.BlockSpec((None, SUBLANES, tn), lambda l, n: (l, 0, n)),
        out_shape=jax.ShapeDtypeStruct((depth, SUBLANES, n6), F32),
        scratch_shapes=[pltpu.VMEM((n_rows, d, LANES), F32)],
        compiler_params=_params(("parallel", "arbitrary")),
        name="ada_table",
    )(cb, ada_w, ada_b.reshape(depth, 1, n6))


class _Layout:
    def __init__(self, b, l, lc, d):
        self.b, self.l, self.lc, self.d = b, l, lc, d
        self.n_lat = b * l
        self.n = b * (l + lc)

    def mod_row(self, tm):
        n_lat_tiles = self.n_lat // tm
        per_batch = self.l // tm
        ctx_row = self.b

        def row(i):
            return jnp.where(i < n_lat_tiles, i // per_batch, ctx_row)
        return row

    def mod_spec(self, layer, chunk, tm, ngrid=1):
        row = self.mod_row(tm)
        if ngrid == 1:
            return pl.BlockSpec((None, None, 1, self.d), lambda i: (layer, row(i), 0, chunk))
        return pl.BlockSpec((None, None, 1, self.d), lambda i, f: (layer, row(i), 0, chunk))


def _norm_kernel(x_ref, g_ref, sh_ref, sc_ref, o_ref):
    o_ref[...] = _norm_mod(x_ref[...], g_ref[...], sc_ref[...], sh_ref[...]).astype(o_ref.dtype)


def _norm_call(lay, x, mod, g, layer):
    tm, d = TM_TOKEN, lay.d
    return pl.pallas_call(
        _norm_kernel,
        grid=(lay.n // tm,),
        in_specs=[pl.BlockSpec((tm, d), lambda i: (i, 0)),
                  pl.BlockSpec((None, 1, d), lambda i: (layer, 0, 0)),
                  lay.mod_spec(layer, 0, tm), lay.mod_spec(layer, 1, tm)],
        out_specs=pl.BlockSpec((tm, d), lambda i: (i, 0)),
        out_shape=jax.ShapeDtypeStruct((lay.n, d), BF16),
        compiler_params=_params(("parallel",)),
        name="norm_mod",
    )(x, g, mod, mod)


def _cast_kernel(w_ref, o_ref):
    o_ref[...] = w_ref[...].astype(o_ref.dtype)


def _cast_layer_call(w, layer):
    _, rows, cols = w.shape
    blk = min(rows, CAST_BLOCK_ELEMS // cols)
    assert rows % blk == 0 and blk % CHUNK == 0
    return pl.pallas_call(
        _cast_kernel,
        grid=(rows // blk,),
        in_specs=[pl.BlockSpec((None, blk, cols), lambda r: (layer, r, 0))],
        out_specs=pl.BlockSpec((None, blk, cols), lambda r: (0, r, 0)),
        out_shape=jax.ShapeDtypeStruct((1, rows, cols), BF16),
        compiler_params=_params(("parallel",)),
        name="cast_bf16",
    )(w)


def _ffn_kernel(x_ref, g_ref, sh_ref, sc_ref, gate_ref, w1_ref, w2_ref, *rest, final, nxt):
    fg_ref = rest[0] if final else None
    gn_ref, shn_ref, scn_ref = rest[:3] if nxt else (None, None, None)
    n_out = 2 if nxt else 1
    o_ref = rest[-2 - n_out]
    an_ref = rest[-3] if nxt else None
    a_ref, acc_ref = rest[-2:]
    f = pl.program_id(1)

    def mlp(a):
        h = jnp.dot(a, w1_ref[...], preferred_element_type=F32)
        h = jnp.maximum(h, 0.0)
        return jnp.dot((h * h).astype(BF16), w2_ref[...], preferred_element_type=F32)

    @pl.when(f == 0)
    def _():
        rows = x_ref.shape[0] // 2
        for r in range(2):
            sl = slice(r * rows, (r + 1) * rows)
            a = _norm_mod(x_ref[sl, :], g_ref[...], sc_ref[...], sh_ref[...]).astype(BF16)
            a_ref[sl, :] = a
            acc_ref[sl, :] = mlp(a)

    @pl.when(f > 0)
    def _():
        acc_ref[...] += mlp(a_ref[...])

    @pl.when(f == pl.num_programs(1) - 1)
    def _():
        out = x_ref[...] + gate_ref[...] * acc_ref[...]
        o_ref[...] = out if fg_ref is None else _rms(out) * fg_ref[...]
        if nxt:
            an_ref[...] = _norm_mod(out, gn_ref[...], scn_ref[...], shn_ref[...]).astype(an_ref.dtype)


def _ffn_call(lay, x, mod, g, w1, w2, layer, wl, n_rows, final_g=None, next_g=None):
    tm, d = TM_TOKEN, lay.d
    assert final_g is None or next_g is None
    extra_specs, extra = [], []
    if final_g is not None:
        extra_specs, extra = [pl.BlockSpec((1, d), lambda i, f: (0, 0))], [final_g]
    if next_g is not None:
        extra_specs = [pl.BlockSpec((None, 1, d), lambda i, f: (layer + 1, 0, 0)),
                       lay.mod_spec(layer + 1, 0, tm, 2), lay.mod_spec(layer + 1, 1, tm, 2)]
        extra = [next_g, mod, mod]
    x_spec = pl.BlockSpec((tm, d), lambda i, f: (i, 0))
    out_specs, out_shape = x_spec, jax.ShapeDtypeStruct((n_rows, d), F32)
    if next_g is not None:
        out_specs, out_shape = [x_spec, x_spec], [out_shape, jax.ShapeDtypeStruct((n_rows, d), BF16)]
    dff = w1.shape[2]
    tf = min(TF_FFN, dff)
    return pl.pallas_call(
        functools.partial(_ffn_kernel, final=final_g is not None, nxt=next_g is not None),
        grid=(n_rows // tm, dff // tf),
        in_specs=[x_spec,
                  pl.BlockSpec((None, 1, d), lambda i, f: (layer, 0, 0)),
                  lay.mod_spec(layer, 3, tm, 2), lay.mod_spec(layer, 4, tm, 2),
                  lay.mod_spec(layer, 5, tm, 2),
                  pl.BlockSpec((None, d, tf), lambda i, f: (wl, 0, f)),
                  pl.BlockSpec((None, tf, d), lambda i, f: (wl, f, 0))] + extra_specs,
        out_specs=out_specs,
        out_shape=out_shape,
        scratch_shapes=[pltpu.VMEM((tm, d), BF16), pltpu.VMEM((tm, d), F32)],
        compiler_params=_params(("parallel", "arbitrary")),
        name="ffn",
    )(x, g, mod, mod, mod, w1, w2, *extra)


def _cmul(ar, ai, br, bi):
    return ar * br - ai * bi, ar * bi + ai * br


def _tile_scan(sr, si, cr, ci, ap_ref, d, reverse):
    rows = lax.broadcasted_iota(jnp.int32, sr.shape, 0)

    def shift(v, k):
        if reverse:
            return jnp.where(rows < SUBLANES - k, pltpu.roll(v, SUBLANES - k, 0), 0.0)
        return jnp.where(rows >= k, pltpu.roll(v, k, 0), 0.0)

    pr, pi = sr, si
    for k in (1, 2, 4):
        akr = ap_ref[d, 0, k:k + 1, :]
        aki = ap_ref[d, 1, k:k + 1, :]
        tr, ti = _cmul(akr, aki, shift(pr, k), shift(pi, k))
        pr, pi = pr + tr, pi + ti
    base = SUBLANES if reverse else 0
    tab_r = ap_ref[d, 0, base:base + SUBLANES, :]
    tab_i = ap_ref[d, 1, base:base + SUBLANES, :]
    tr, ti = _cmul(tab_r, tab_i, cr, ci)
    hr, hi = shift(pr, 1) + tr, shift(pi, 1) + ti
    a8r = ap_ref[d, 0, 2 * SUBLANES:2 * SUBLANES + 1, :]
    a8i = ap_ref[d, 1, 2 * SUBLANES:2 * SUBLANES + 1, :]
    tr, ti = _cmul(a8r, a8i, cr, ci)
    last = 0 if reverse else SUBLANES - 1
    return hr, hi, pr[last:last + 1, :] + tr, pi[last:last + 1, :] + ti


def _split_bf16(v):
    hi = v.astype(BF16)
    return hi, (v - hi.astype(F32)).astype(BF16)


def _s5_core_kernel(*refs, nb, kl, kc):
    al_refs = refs[:nb]
    ac_refs = refs[nb:2 * nb]
    lamt_ref, bt_ref, laml_ref, ct_ref, ap_ref, dsk_ref, yl_ref, yc_ref = refs[2 * nb:2 * nb + 8]
    x_s, w_s, xc_s, wintra_s, bd_s, s_s, y_s = refs[2 * nb + 8:]
    gpl = LANES // CHUNK
    ns = x_s.shape[1] // LANES
    sw = s_s.shape[2]
    half = sw // 2
    lat_rows = nb * kl
    p_rows = ct_ref.shape[2]
    lane_group_p = lax.broadcasted_iota(jnp.int32, (p_rows, LANES), 1) // CHUNK

    for s in range(ns):
        for b in range(nb):
            x_s[b * kl:(b + 1) * kl, s * LANES:(s + 1) * LANES] = al_refs[b][s]
            x_s[lat_rows + b * kc:lat_rows + (b + 1) * kc, s * LANES:(s + 1) * LANES] = ac_refs[b][s]
    x = x_s[...]

    zero_tile = jnp.zeros((CHUNK, LANES), BF16)
    n_lt = half // LANES
    row_i = lax.broadcasted_iota(jnp.int32, (LANES, LANES), 0)
    lane_i = lax.broadcasted_iota(jnp.int32, (LANES, LANES), 1)
    same_group = row_i // CHUNK == lane_i // CHUNK
    even_lo = ((row_i // CHUNK) % 2 == 0) & (lane_i < LANES // 2)
    odd_hi = ((row_i // CHUNK) % 2 == 1) & (lane_i >= LANES // 2)
    k0 = jnp.where(row_i == lane_i, dsk_ref[...], 0.0)
    for d in range(2):
        cur, swapped = bt_ref[d, 0], bt_ref[d, 1]
        lam_a, lam_b = lamt_ref[d, 0], lamt_ref[d, 1]
        for e in range(ns):
            s = ns - 1 - e if d == 0 else e
            hi, lo = _split_bf16(cur)
            xc_s[0, s * LANES:(s + 1) * LANES, :] = hi
            xc_s[1, s * LANES:(s + 1) * LANES, :] = lo
            re_pos = jnp.where(even_lo, cur, jnp.where(odd_hi, swapped, 0.0)).astype(BF16)
            im_pos = jnp.where(even_lo, swapped, jnp.where(odd_hi, cur, 0.0)).astype(BF16)
            for g in range(gpl):
                r0 = g * CHUNK
                tiles = ([re_pos[r0:r0 + CHUNK, :] if lt == g // 2 else zero_tile for lt in range(n_lt)]
                         + [im_pos[r0:r0 + CHUNK, :] if lt == g // 2 else zero_tile for lt in range(n_lt)])
                w_s[d, s * LANES + r0:s * LANES + r0 + CHUNK, :] = jnp.concatenate(tiles, axis=-1)
            cur, swapped = lam_a * cur + lam_b * swapped, lam_a * swapped - lam_b * cur
        s_s[d] = jnp.dot(x, w_s[d], preferred_element_type=F32)
        c_hi, c_lo = _split_bf16(jnp.concatenate([ct_ref[d, 0], -ct_ref[d, 1]], axis=0))
        kall = (jnp.dot(xc_s[0], c_hi, preferred_element_type=F32)
                + jnp.dot(xc_s[0], c_lo, preferred_element_type=F32)
                + jnp.dot(xc_s[1], c_hi, preferred_element_type=F32))
        for e in range(ns):
            s = ns - 1 - e if d == 0 else e
            ktile = jnp.where(same_group, kall[s * LANES:(s + 1) * LANES, :], 0.0)
            if e == 0:
                k0 = k0 + ktile
            else:
                bd_s[ns - 1 + (e if d == 0 else -e)] = ktile.astype(BF16)
    bd_s[ns - 1] = k0.astype(BF16)

    n_ctx_tiles = kc // SUBLANES
    n_lat_tiles = kl // SUBLANES

    for s in range(ns):
        for t in range(ns):
            wintra_s[s * LANES:(s + 1) * LANES, t * LANES:(t + 1) * LANES] = bd_s[t - s + ns - 1]
    y_s[...] = jnp.dot(x, wintra_s[...], preferred_element_type=F32)

    carry = [jnp.zeros((1, half), F32)] * (4 * nb)
    for i in range(n_ctx_tiles + n_lat_tiles):
        for d in range(2):
            for b in range(nb):
                cr, ci = carry[2 * (d * nb + b)], carry[2 * (d * nb + b) + 1]
                if i < n_ctx_tiles:
                    row = lat_rows + b * kc + (i if d == 0 else n_ctx_tiles - 1 - i) * SUBLANES
                else:
                    il = i - n_ctx_tiles
                    row = b * kl + (il if d == 0 else n_lat_tiles - 1 - il) * SUBLANES
                sr = s_s[d, row:row + SUBLANES, 0:half]
                si = s_s[d, row:row + SUBLANES, half:sw]
                hr, hi, cr, ci = _tile_scan(sr, si, cr, ci, ap_ref, d, d == 1)
                s_s[d, row:row + SUBLANES, 0:half] = hr
                s_s[d, row:row + SUBLANES, half:sw] = hi
                carry[2 * (d * nb + b)], carry[2 * (d * nb + b) + 1] = cr, ci

    hcols = (ns * LANES) // 2
    for d in range(2):
        lam_r, lam_i = laml_ref[d, 0], laml_ref[d, 1]
        cur_r, cur_i = _cmul(lam_r, lam_i, ct_ref[d, 0], ct_ref[d, 1])
        for e in range(ns):
            t = e if d == 0 else ns - 1 - e
            c0 = t * LANES
            rbase = (c0 // hcols) * sw
            cc = c0 % hcols
            for part, ctile in ((0, cur_r), (1, -cur_i)):
                for g in range(gpl):
                    r0 = rbase + part * half + g * p_rows
                    w_s[d, r0:r0 + p_rows, cc:cc + LANES] = jnp.where(
                        lane_group_p == g, ctile, 0.0).astype(BF16)
            cur_r, cur_i = _cmul(lam_r, lam_i, cur_r, cur_i)
    for d in range(2):
        h = s_s[d].astype(BF16)
        for c in range(2):
            y_s[:, c * hcols:(c + 1) * hcols] += jnp.dot(
                h, w_s[d, c * sw:(c + 1) * sw, :], preferred_element_type=F32)

    for t in range(ns):
        for b in range(nb):
            yl_ref[b * ns + t] = y_s[b * kl:(b + 1) * kl, t * LANES:(t + 1) * LANES]
            yc_ref[b * ns + t] = y_s[lat_rows + b * kc:lat_rows + (b + 1) * kc, t * LANES:(t + 1) * LANES]


def _s5_core_call(lay, a, tabs, layer):
    lamt, bt, laml, ct, apow, dsk = tabs
    nb, d = lay.b, lay.d
    kl, kc = lay.l // CHUNK, lay.lc // CHUNK
    nj = d // LANES
    rows = nb * (kl + kc)
    sw = 2 * (LANES // CHUNK) * ct.shape[-2]
    assert 2 * sw == CHUNK * LANES
    a_l = a.reshape(lay.n // kl, kl, d)
    a_c = a.reshape(lay.n // kc, kc, d)
    ctx_blk0 = (nb * kl) // kc

    def tab_spec(t):
        return pl.BlockSpec((None, 2, t.shape[2], None) + t.shape[4:], lambda j: (layer, 0, 0, j, 0, 0))
    in_specs = ([pl.BlockSpec((CHUNK, kl, LANES), functools.partial(lambda j, b: (b, 0, j), b=b))
                 for b in range(nb)]
                + [pl.BlockSpec((CHUNK, kc, LANES), functools.partial(lambda j, b: (ctx_blk0 + b, 0, j), b=b))
                   for b in range(nb)]
                + [tab_spec(lamt), tab_spec(bt), tab_spec(laml), tab_spec(ct), tab_spec(apow),
                   pl.BlockSpec((None, None, 1, LANES), lambda j: (layer, j, 0, 0))])
    y_l, y_c = pl.pallas_call(
        functools.partial(_s5_core_kernel, nb=nb, kl=kl, kc=kc),
        grid=(nj,),
        in_specs=in_specs,
        out_specs=[pl.BlockSpec((nb * CHUNK, kl, LANES), lambda j: (0, 0, j)),
                   pl.BlockSpec((nb * CHUNK, kc, LANES), lambda j: (0, 0, j))],
        out_shape=[jax.ShapeDtypeStruct((nb * CHUNK, kl, d), F32),
                   jax.ShapeDtypeStruct((nb * CHUNK, kc, d), F32)],
        scratch_shapes=[pltpu.VMEM((rows, CHUNK * LANES), BF16),
                        pltpu.VMEM((2, CHUNK * LANES, sw), BF16),
                        pltpu.VMEM((2, CHUNK * LANES, LANES), BF16),
                        pltpu.VMEM((CHUNK * LANES, CHUNK * LANES), BF16),
                        pltpu.VMEM((2 * CHUNK - 1, LANES, LANES), BF16),
                        pltpu.VMEM((2, rows, sw), F32),
                        pltpu.VMEM((rows, CHUNK * LANES), F32)],
        compiler_params=_params(("parallel",)),
        name="s5_core",
    )(*([a_l] * nb + [a_c] * nb + [lamt, bt, laml, ct, apow, dsk]))
    return y_l.reshape(nb * lay.l, d), y_c.reshape(nb * lay.lc, d)


def _s5_tables(lam_re, lam_im, log_dt, b_re, b_im, c_re, c_im, d_skip):
    nl, _, ng, p = lam_re.shape
    grp = b_re.shape[-1]
    gpl = LANES // grp
    nj = ng // gpl
    dt = jnp.exp(log_dt.astype(F32))[..., None]
    lr, li = lam_re.astype(F32), lam_im.astype(F32)

    mag = jnp.exp(lr * dt)
    lbr, lbi = mag * jnp.cos(li * dt), mag * jnp.sin(li * dt)
    nr = lbr - 1.0
    den = lr * lr + li * li
    fr = (nr * lr + lbi * li) / den
    fi = (lbi * lr - nr * li) / den
    bbr = fr[..., None] * b_re - fi[..., None] * b_im
    bbi = fr[..., None] * b_im + fi[..., None] * b_re

    def rows_gc(re, im):
        v = jnp.concatenate([re, im], axis=-1)[:, :, :, None, :]
        return jnp.broadcast_to(v, (nl, 2, ng, grp, 2 * p)).reshape(nl, 2, nj, gpl * grp, 2 * p)

    def rows_p(v):
        v = v.reshape(nl, 2, nj, gpl, grp, p)
        return jnp.transpose(v, (0, 1, 2, 5, 3, 4)).reshape(nl, 2, nj, p, gpl * grp)

    lamt = jnp.stack([rows_gc(lbr, lbr), rows_gc(-lbi, lbi)], axis=2)
    bbr_t, bbi_t = jnp.swapaxes(bbr, -1, -2), jnp.swapaxes(bbi, -1, -2)
    bt = jnp.stack([jnp.concatenate([bbr_t, bbi_t], axis=-1), jnp.concatenate([bbi_t, bbr_t], axis=-1)], axis=2)
    bt = bt.reshape(nl, 2, 2, nj, gpl * grp, 2 * p)
    lam_bc = lambda v: jnp.broadcast_to(v[:, :, :, None, :], (nl, 2, ng, grp, p))
    laml = jnp.stack([rows_p(lam_bc(lbr)), rows_p(lam_bc(lbi))], axis=2)
    ct = jnp.stack([rows_p(c_re.astype(F32)), rows_p(c_im.astype(F32))], axis=2)

    m = (CHUNK * jnp.arange(SUBLANES + 1, dtype=F32))[:, None, None, None, None]
    amag = jnp.exp(lr * dt * m)

    def lay_a(v):
        tab = jnp.concatenate([v[:SUBLANES], v[:SUBLANES][::-1], v[SUBLANES:],
                               jnp.zeros((SUBLANES - 1,) + v.shape[1:], F32)], axis=0)
        tab = tab.reshape(3 * SUBLANES, nl, 2, nj, gpl * p)
        return jnp.transpose(tab, (1, 2, 3, 0, 4))
    apow = jnp.stack([lay_a(amag * jnp.cos(li * dt * m)), lay_a(amag * jnp.sin(li * dt * m))], axis=2)
    dsk = d_skip.astype(F32).reshape(nl, nj, 1, LANES)
    return lamt, bt, laml, ct, apow, dsk


def _glu_kernel(yl_ref, yc_ref, x_ref, gate_ref, w_ref, b_ref, o_ref, *, n_lat_tiles):
    i = pl.program_id(0)
    y = jnp.where(i < n_lat_tiles, yl_ref[...], yc_ref[...])
    z = _gelu_tanh(y)
    u = jnp.dot(z.astype(BF16), w_ref[...], preferred_element_type=F32) + b_ref[...]
    o_ref[...] = x_ref[...] + gate_ref[...] * (z * jax.nn.sigmoid(u))


def _glu_call(lay, y_l, y_c, x, mod, w, bias, layer, j):
    tm, d = TM_TOKEN, lay.d
    nlt = lay.n_lat // tm
    return pl.pallas_call(
        functools.partial(_glu_kernel, n_lat_tiles=nlt),
        grid=(lay.n // tm,),
        in_specs=[pl.BlockSpec((tm, d), lambda i: (jnp.minimum(i, nlt - 1), 0)),
                  pl.BlockSpec((tm, d), lambda i: (jnp.maximum(i - nlt, 0), 0)),
                  pl.BlockSpec((tm, d), lambda i: (i, 0)),
                  lay.mod_spec(layer, 2, tm),
                  pl.BlockSpec((None, d, d), lambda i: (j, 0, 0)),
                  pl.BlockSpec((None, 1, d), lambda i: (j, 0, 0))],
        out_specs=pl.BlockSpec((tm, d), lambda i: (i, 0)),
        out_shape=jax.ShapeDtypeStruct((lay.n, d), F32),
        compiler_params=_params(("parallel",)),
        name="s5_glu",
    )(y_l, y_c, x, mod, w, bias)


def _rope(v, cos, sin):
    return v * cos + pltpu.roll(v, LANES // 2, 1) * sin


def _mla_proj_kernel(x_ref, g_ref, sh_ref, sc_ref, cos_ref, sin_ref, wdq_ref, qn_ref, wuq_ref,
                     wdkv_ref, kvn_ref, wukv_ref, q_ref, k_ref, v_ref, *, heads, scale):
    a = _norm_mod(x_ref[...], g_ref[...], sc_ref[...], sh_ref[...]).astype(BF16)
    cos, sin = cos_ref[...], sin_ref[...]
    hd = heads * LANES
    cq = jnp.dot(a, wdq_ref[...], preferred_element_type=F32)
    cq = (_rms(cq) * qn_ref[...]).astype(BF16)
    q = jnp.dot(cq, wuq_ref[...], preferred_element_type=F32)
    ckv = jnp.dot(a, wdkv_ref[...], preferred_element_type=F32)
    kvl = ckv.shape[1] - LANES
    ckvn = (_rms(ckv[:, :kvl]) * kvn_ref[...]).astype(BF16)
    kpe = _rope(ckv[:, kvl:], cos, sin).astype(BF16)
    kv = jnp.dot(ckvn, wukv_ref[...], preferred_element_type=F32)
    for h in range(heads):
        lo, hi_ = h * LANES, (h + 1) * LANES
        q_ref[h, :, 0:LANES] = (q[:, lo:hi_] * scale).astype(BF16)
        q_ref[h, :, LANES:2 * LANES] = (_rope(q[:, hd + lo:hd + hi_], cos, sin) * scale).astype(BF16)
        k_ref[h, :, 0:LANES] = kv[:, lo:hi_].astype(BF16)
        k_ref[h, :, LANES:2 * LANES] = kpe
        v_ref[h] = kv[:, hd + lo:hd + hi_].astype(BF16)


def _mla_proj_call(lay, x, mod, g, cos_t, sin_t, wts, layer, j, heads, scale):
    wdq, qn, wuq, wdkv, kvn, wukv = wts
    tm, d = TM_PROJ, lay.d
    full = lambda arr: pl.BlockSpec((None,) + arr.shape[1:], lambda i: (j, 0, 0),
                                    pipeline_mode=pl.Buffered(1))
    return pl.pallas_call(
        functools.partial(_mla_proj_kernel, heads=heads, scale=scale),
        grid=(lay.n // tm,),
        in_specs=[pl.BlockSpec((tm, d), lambda i: (i, 0)),
                  pl.BlockSpec((None, 1, d), lambda i: (layer, 0, 0)),
                  lay.mod_spec(layer, 0, tm), lay.mod_spec(layer, 1, tm),
                  pl.BlockSpec((tm, LANES), lambda i: (i, 0)),
                  pl.BlockSpec((tm, LANES), lambda i: (i, 0)),
                  full(wdq), full(qn), full(wuq), full(wdkv), full(kvn), full(wukv)],
        out_specs=[pl.BlockSpec((heads, tm, 2 * LANES), lambda i: (0, i, 0)),
                   pl.BlockSpec((heads, tm, 2 * LANES), lambda i: (0, i, 0)),
                   pl.BlockSpec((heads, tm, LANES), lambda i: (0, i, 0))],
        out_shape=[jax.ShapeDtypeStruct((heads, lay.n, 2 * LANES), BF16),
                   jax.ShapeDtypeStruct((heads, lay.n, 2 * LANES), BF16),
                   jax.ShapeDtypeStruct((heads, lay.n, LANES), BF16)],
        compiler_params=_params(("parallel",)),
        name="mla_proj",
    )(x, g, mod, mod, cos_t, sin_t, wdq, qn, wuq, wdkv, kvn, wukv)


def _attn_kernel(*refs, tk, n_kv, n_cast=0):
    q_ref, kc_ref, vc_ref = refs[:3]
    kl_ref, vl_ref = (refs[3], refs[4]) if n_kv else (None, None)
    if n_cast:
        for src, dst in zip(refs[5:5 + n_cast], refs[-n_cast:]):
            dst[...] = src[...].astype(dst.dtype)
    o_ref = refs[-1 - n_cast]
    nt = (((1,), (1,)), ((), ()))
    hps = q_ref.shape[0]
    qs, ms, ls, accs = [], [], [], []
    for h in range(hps):
        q = q_ref[h]
        s = lax.dot_general(q, kc_ref[h], nt, preferred_element_type=F32)
        m = jnp.max(s, axis=-1, keepdims=True)
        p = jnp.exp2(s - m)
        qs.append(q)
        ms.append(m)
        ls.append(jnp.sum(p, axis=-1, keepdims=True))
        accs.append(jnp.dot(p.astype(BF16), vc_ref[h], preferred_element_type=F32))
    for i in range(n_kv):
        for h in range(hps):
            s = lax.dot_general(qs[h], kl_ref[h, i * tk:(i + 1) * tk, :], nt, preferred_element_type=F32)
            m_new = jnp.maximum(ms[h], jnp.max(s, axis=-1, keepdims=True))
            alpha = jnp.exp2(ms[h] - m_new)
            p = jnp.exp2(s - m_new)
            ls[h] = alpha * ls[h] + jnp.sum(p, axis=-1, keepdims=True)
            accs[h] = alpha * accs[h] + jnp.dot(p.astype(BF16), vl_ref[h, i * tk:(i + 1) * tk, :],
                                                preferred_element_type=F32)
            ms[h] = m_new
    for h in range(hps):
        o_ref[:, h * LANES:(h + 1) * LANES] = (accs[h] / ls[h]).astype(o_ref.dtype)


def _attn_call(lay, q, k, v, heads, need_ctx, cast):
    nb, l, lc = lay.b, lay.l, lay.lc
    hps = ATTN_HEADS
    tq = min(TQ_ATTN, l)
    tk = min(TK_ATTN, l)
    nlq = l // tq
    nh = heads // hps
    ctx_blk0 = (nb * l) // lc
    ctx_specs = [pl.BlockSpec((hps, lc, 2 * LANES), lambda b, h, qi: (h, ctx_blk0 + b, 0)),
                 pl.BlockSpec((hps, lc, LANES), lambda b, h, qi: (h, ctx_blk0 + b, 0))]
    arrays, first, count = cast
    steps = nb * nh * nlq
    cast_in, cast_specs, cast_out_specs, cast_shapes = [], [], [], []
    for w in arrays:
        rows, cols = w.shape[1:]
        blk = count * rows // steps
        assert blk * steps == count * rows and blk % CHUNK == 0
        blk0 = first * rows // blk
        cast_in.append(w.reshape(-1, cols))
        cast_specs.append(pl.BlockSpec(
            (blk, cols), functools.partial(lambda b, h, qi, o: (o + (b * nh + h) * nlq + qi, 0), o=blk0)))
        cast_out_specs.append(pl.BlockSpec((blk, cols), lambda b, h, qi: ((b * nh + h) * nlq + qi, 0)))
        cast_shapes.append(jax.ShapeDtypeStruct((count * rows, cols), BF16))
    outs = pl.pallas_call(
        functools.partial(_attn_kernel, tk=tk, n_kv=l // tk, n_cast=len(arrays)),
        grid=(nb, nh, nlq),
        in_specs=[pl.BlockSpec((hps, tq, 2 * LANES), lambda b, h, qi: (h, b * nlq + qi, 0))] + ctx_specs
                 + [pl.BlockSpec((hps, l, 2 * LANES), lambda b, h, qi: (h, b, 0)),
                    pl.BlockSpec((hps, l, LANES), lambda b, h, qi: (h, b, 0))] + cast_specs,
        out_specs=[pl.BlockSpec((tq, hps * LANES), lambda b, h, qi: (b * nlq + qi, h))] + cast_out_specs,
        out_shape=[jax.ShapeDtypeStruct((lay.n_lat, heads * LANES), BF16)] + cast_shapes,
        compiler_params=_params(("parallel", "parallel", "arbitrary")),
        name="mla_attn",
    )(q, k, v, k, v, *cast_in)
    o_lat = outs[0]
    converted = [o.reshape((count,) + w.shape[1:]) for o, w in zip(outs[1:], arrays)]
    if not need_ctx:
        return o_lat, None, converted
    o_ctx = pl.pallas_call(
        functools.partial(_attn_kernel, tk=tk, n_kv=0),
        grid=(nb, heads // hps, 1),
        in_specs=[pl.BlockSpec((hps, lc, 2 * LANES), lambda b, h, qi: (h, ctx_blk0 + b, 0))] + ctx_specs,
        out_specs=pl.BlockSpec((lc, hps * LANES), lambda b, h, qi: (b, h)),
        out_shape=jax.ShapeDtypeStruct((nb * lc, heads * LANES), BF16),
        compiler_params=_params(("parallel", "parallel", "arbitrary")),
        name="mla_attn_ctx",
    )(q, k, v)
    return o_lat, o_ctx, converted


def _oproj_kernel(*refs, n_lat_tiles):
    ol_ref = refs[0]
    oc_ref = refs[1] if len(refs) == 6 else None
    x_ref, gate_ref, w_ref, out_ref = refs[-4:]
    o = ol_ref[...]
    if oc_ref is not None:
        o = jnp.where(pl.program_id(0) < n_lat_tiles, o, oc_ref[...])
    out_ref[...] = x_ref[...] + gate_ref[...] * jnp.dot(o, w_ref[...], preferred_element_type=F32)


def _oproj_call(lay, o_lat, o_ctx, x, mod, w, layer, j, n_rows):
    tm, d = TM_TOKEN, lay.d
    nlt = lay.n_lat // tm
    hd = o_lat.shape[1]
    o_specs = [pl.BlockSpec((tm, hd), lambda i: (jnp.minimum(i, nlt - 1), 0))]
    if o_ctx is not None:
        o_specs.append(pl.BlockSpec((tm, hd), lambda i: (jnp.maximum(i - nlt, 0), 0)))
    return pl.pallas_call(
        functools.partial(_oproj_kernel, n_lat_tiles=nlt),
        grid=(n_rows // tm,),
        in_specs=o_specs + [pl.BlockSpec((tm, d), lambda i: (i, 0)),
                            lay.mod_spec(layer, 2, tm),
                            pl.BlockSpec((None, hd, d), lambda i: (j, 0, 0))],
        out_specs=pl.BlockSpec((tm, d), lambda i: (i, 0)),
        out_shape=jax.ShapeDtypeStruct((n_rows, d), F32),
        compiler_params=_params(("parallel",)),
        name="mla_oproj",
    )(*([o_lat] + ([] if o_ctx is None else [o_ctx]) + [x, mod, w]))


def _to_chunk_major(t):
    b, n, d = t.shape
    return jnp.transpose(t.reshape(b, n // CHUNK, CHUNK, d), (0, 2, 1, 3)).reshape(b * n, d)


def _from_chunk_major(t, b):
    n, d = t.shape[0] // b, t.shape[1]
    return jnp.transpose(t.reshape(b, CHUNK, n // CHUNK, d), (0, 2, 1, 3)).reshape(b, n, d)


def _pad_rope_cols(w):
    nf = w.shape[-1] // 4
    w4 = w.reshape(w.shape[:-1] + (2, 2, nf))
    z = jnp.zeros(w.shape[:-1] + (LANES // 2 - 2 * nf,), w.dtype)
    x1 = w4[..., 0, :].reshape(w.shape[:-1] + (2 * nf,))
    x2 = w4[..., 1, :].reshape(w.shape[:-1] + (2 * nf,))
    return jnp.concatenate([x1, z, x2, z], axis=-1)


def _rope_tables(lay, rope_dim):
    nf = rope_dim // 4
    pos = jnp.arange(lay.l)
    inv = ROPE_THETA ** (-jnp.arange(nf, dtype=F32) / nf)
    row = (pos // GRID_W).astype(F32)
    col = (pos % GRID_W).astype(F32)
    ang = jnp.concatenate([row[:, None] * inv, col[:, None] * inv], axis=1)
    z = jnp.zeros((lay.l, LANES // 2 - 2 * nf), F32)
    cos_l = jnp.concatenate([jnp.cos(ang), z, jnp.cos(ang), z], axis=1)
    sin_l = jnp.concatenate([-jnp.sin(ang), z, jnp.sin(ang), z], axis=1)
    cos_l = _to_chunk_major(jnp.broadcast_to(cos_l[None], (lay.b, lay.l, LANES)))
    sin_l = _to_chunk_major(jnp.broadcast_to(sin_l[None], (lay.b, lay.l, LANES)))
    n_ctx = lay.b * lay.lc
    cos_t = jnp.concatenate([cos_l, jnp.ones((n_ctx, LANES), F32)], axis=0)
    sin_t = jnp.concatenate([sin_l, jnp.zeros((n_ctx, LANES), F32)], axis=0)
    return cos_t, sin_t


def _mla_weights(w_dq, q_norm, w_uq, w_dkv, kv_norm, w_ukv, heads, nope, rope, vdim):
    n, ql = w_uq.shape[:2]
    kvl = kv_norm.shape[1]
    uq = w_uq.reshape(n, ql, heads, nope + rope)
    uq_pe = _pad_rope_cols(uq[..., nope:]).reshape(n, ql, heads * LANES)
    wuq = jnp.concatenate([uq[..., :nope].reshape(n, ql, heads * nope), uq_pe], axis=-1)
    wdkv = jnp.concatenate([w_dkv[..., :kvl], _pad_rope_cols(w_dkv[..., kvl:])], axis=-1)
    ukv = w_ukv.reshape(n, kvl, heads, nope + vdim)
    wukv = jnp.concatenate([ukv[..., :nope].reshape(n, kvl, heads * nope),
                            ukv[..., nope:].reshape(n, kvl, heads * vdim)], axis=-1)
    return (w_dq.astype(BF16), q_norm.reshape(n, 1, ql).astype(F32), wuq.astype(BF16),
            wdkv.astype(BF16), kv_norm.reshape(n, 1, kvl).astype(F32), wukv.astype(BF16))


def kernel(x, c, ctx, c_ctx, ada_w, ada_b, norm1_g, norm2_g, final_norm_g, s5_lam_re, s5_lam_im, s5_log_dt, s5_b_re, s5_b_im, s5_c_re, s5_c_im, s5_d, s5_w_glu, s5_b_glu, mla_w_dq, mla_q_norm, mla_w_uq, mla_w_dkv, mla_kv_norm, mla_w_ukv, mla_w_o, ffn_w1, ffn_w2):
    b, l, d = x.shape
    lc = ctx.shape[1]
    depth = ada_w.shape[0]
    lay = _Layout(b, l, lc, d)

    kvl = mla_kv_norm.shape[1]
    rope = mla_w_dkv.shape[2] - kvl
    heads = (mla_w_uq.shape[2] + mla_w_o.shape[1] - mla_w_ukv.shape[2]) // rope
    vdim = mla_w_o.shape[1] // heads
    nope = mla_w_uq.shape[2] // heads - rope
    assert nope == LANES and vdim == LANES and 4 * (rope // 4) == rope and rope <= LANES
    assert b + 1 <= SUBLANES and s5_b_re.shape[-1] == CHUNK
    scale = (nope + rope) ** -0.5 * math.log2(math.e)

    xs =jnp.concatenate([_to_chunk_major(x), _to_chunk_major(ctx)], axis=0)

    cc = jnp.concatenate([c, c_ctx[None]], axis=0)
    mod = _ada_table(cc, ada_w, ada_b).reshape(depth, SUBLANES, 1, 6 * d)
    g1 = norm1_g.reshape(depth, 1, d)
    g2 = norm2_g.reshape(depth, 1, d)
    cos_t, sin_t = _rope_tables(lay, rope)

    tabs = _s5_tables(s5_lam_re, s5_lam_im, s5_log_dt, s5_b_re, s5_b_im, s5_c_re, s5_c_im, s5_d)
    wts = _mla_weights(mla_w_dq, mla_q_norm, mla_w_uq, mla_w_dkv, mla_kv_norm, mla_w_ukv,
                       heads, nope, rope, vdim)
    w_glu, b_glu = s5_w_glu.astype(BF16), s5_b_glu.reshape(-1, 1, d)
    w_o = mla_w_o.astype(BF16)
    mlp_w = {0: (_cast_layer_call(ffn_w1, 0), _cast_layer_call(ffn_w2, 0), 0)}

    a_next = None
    for i in range(depth):
        j = i // 2
        n_rows = lay.n_lat if i == depth - 1 else lay.n
        if i % 2 == 0:
            if a_next is None:
                a_next = _norm_call(lay, xs, mod, g1, i)
            y_l, y_c = _s5_core_call(lay, a_next, tabs, j)
            xs = _glu_call(lay, y_l, y_c, xs, mod, w_glu, b_glu, i, j)
        else:
            q, k, v = _mla_proj_call(lay, xs, mod, g1, cos_t, sin_t, wts, i, j, heads, scale)
            count = min(2, depth - i)
            o_lat, o_ctx, (w1c, w2c) = _attn_call(lay, q, k, v, heads, n_rows == lay.n,
                                                  ([ffn_w1, ffn_w2], i, count))
            for t in range(count):
                mlp_w[i + t] = (w1c, w2c, t)
            xs = _oproj_call(lay, o_lat, o_ctx, xs, mod, w_o, i, j, n_rows)
        w1, w2, wl = mlp_w[i]
        final_g = final_norm_g.reshape(1, d) if i == depth - 1 else None
        feeds_s5 = i + 1 < depth and (i + 1) % 2 == 0
        res = _ffn_call(lay, xs, mod, g2, w1, w2, i, wl, n_rows, final_g, g1 if feeds_s5 else None)
        xs, a_next = res if feeds_s5 else (res, None)

    return _from_chunk_major(xs, b)
```

```python
import functools
import math

import jax
import jax.numpy as jnp
from jax import lax
from jax.experimental import pallas as pl
from jax.experimental.pallas import tpu as pltpu

F32 = jnp.float32
BF16 = jnp.bfloat16

EPS = 1e-6
GRID_W = 64
ROPE_THETA = 10000.0

LANES = 128
SUBLANES = 8
CHUNK = 16
VMEM_LIMIT = 56 * 1024 * 1024

TM_TOKEN = 512
TM_PROJ = 512
TQ_ATTN = 1024
ATTN_HEADS = 2
TK_ATTN = 256
TF_FFN = 1024
TN_ADA = 1536
CAST_BLOCK_ELEMS = 2 * 1024 * 1024


def _params(sem):
    return pltpu.CompilerParams(dimension_semantics=sem, vmem_limit_bytes=VMEM_LIMIT)


def _rms(x):
    return x * lax.rsqrt(jnp.mean(x * x, axis=-1, keepdims=True) + EPS)


def _norm_mod(x, g, sc, sh):
    return (_rms(x) * g) * (1.0 + sc) + sh


def _gelu_tanh(y):
    c = math.sqrt(2.0 / math.pi)
    return 0.5 * y * (1.0 + jnp.tanh(c * (y + 0.044715 * (y * y * y))))


def _ada_kernel(cb_ref, w_ref, b_ref, o_ref, sb_s):
    n_rows, d, _ = cb_ref.shape
    n_lt = w_ref.shape[1] // LANES
    @pl.when(pl.program_id(1) == 0)
    def _():
        cb = cb_ref[...]
        sb_s[...] = cb * jax.nn.sigmoid(cb)
    rows = lax.broadcasted_iota(jnp.int32, (SUBLANES, 1), 0)

    def body(kb, accs):
        k0 = pl.multiple_of(kb * SUBLANES, SUBLANES)
        w = w_ref[pl.ds(k0, SUBLANES), :]
        return tuple(acc + w * jnp.concatenate([sb_s[r, pl.ds(k0, SUBLANES), :]] * n_lt, axis=1)
                     for r, acc in enumerate(accs))
    accs = lax.fori_loop(0, d // SUBLANES, body, (jnp.zeros(o_ref.shape, F32),) * n_rows, unroll=4)
    out = jnp.broadcast_to(b_ref[...], o_ref.shape)
    for r, acc in enumerate(accs):
        out = out + jnp.where(rows == r, jnp.sum(acc, axis=0, keepdims=True), 0.0)
    o_ref[...] = out


def _ada_table(cc, ada_w, ada_b):
    depth, d, n6 = ada_w.shape
    n_rows = cc.shape[0]
    tn = min(TN_ADA, n6)
    cb = jnp.broadcast_to(cc[:, :, None], (n_rows, d, LANES))
    return pl.pallas_call(
        _ada_kernel,
        grid=(depth, n6 // tn),
        in_specs=[pl.BlockSpec((n_rows, d, LANES), lambda l, n: (0, 0, 0)),
                  pl.BlockSpec((None, d, tn), lambda l, n: (l, 0, n)),
                  pl.BlockSpec((None, 1, tn), lambda l, n: (l, 0, n))],
        out_specs=pl.BlockSpec((None, SUBLANES, tn), lambda l, n: (l, 0, n)),
        out_shape=jax.ShapeDtypeStruct((depth, SUBLANES, n6), F32),
        scratch_shapes=[pltpu.VMEM((n_rows, d, LANES), F32)],
        compiler_params=_params(("parallel", "arbitrary")),
        name="ada_table",
    )(cb, ada_w, ada_b.reshape(depth, 1, n6))


class _Layout:
    def __init__(self, b, l, lc, d):
        self.b, self.l, self.lc, self.d = b, l, lc, d
        self.n_lat = b * l
        self.n = b * (l + lc)

    def mod_row(self, tm):
        n_lat_tiles = self.n_lat // tm
        per_batch = self.l // tm
        ctx_row = self.b

        def row(i):
            return jnp.where(i < n_lat_tiles, i // per_batch, ctx_row)
        return row

    def mod_spec(self, layer, chunk, tm, ngrid=1):
        row = self.mod_row(tm)
        if ngrid == 1:
            return pl.BlockSpec((None, None, 1, self.d), lambda i: (layer, row(i), 0, chunk))
        return pl.BlockSpec((None, None, 1, self.d), lambda i, f: (layer, row(i), 0, chunk))


def _split_specs(lay, tm):
    nlt, d = lay.n_lat // tm, lay.d
    return [pl.BlockSpec((tm, d), lambda i: (jnp.minimum(i, nlt - 1), 0)),
            pl.BlockSpec((tm, d), lambda i: (jnp.maximum(i - nlt, 0), 0))]


def _pick(lat_ref, ctx_ref, n_lat_tiles):
    return jnp.where(pl.program_id(0) < n_lat_tiles, lat_ref[...], ctx_ref[...])


def _norm_kernel(xl_ref, xc_ref, g_ref, sh_ref, sc_ref, o_ref, *, n_lat_tiles):
    x = _pick(xl_ref, xc_ref, n_lat_tiles)
    o_ref[...] = _norm_mod(x, g_ref[...], sc_ref[...], sh_ref[...]).astype(o_ref.dtype)


def _norm_call(lay, x_lat, x_ctx, mod, g, layer):
    tm, d = TM_TOKEN, lay.d
    return pl.pallas_call(
        functools.partial(_norm_kernel, n_lat_tiles=lay.n_lat // tm),
        grid=(lay.n // tm,),
        in_specs=_split_specs(lay, tm) + [pl.BlockSpec((None, 1, d), lambda i: (layer, 0, 0)),
                                          lay.mod_spec(layer, 0, tm), lay.mod_spec(layer, 1, tm)],
        out_specs=pl.BlockSpec((tm, d), lambda i: (i, 0)),
        out_shape=jax.ShapeDtypeStruct((lay.n, d), BF16),
        compiler_params=_params(("parallel",)),
        name="norm_mod",
    )(x_lat, x_ctx, g, mod, mod)


def _cast_kernel(w_ref, o_ref):
    o_ref[...] = w_ref[...].astype(o_ref.dtype)


def _cast_layer_call(w, layer):
    _, rows, cols = w.shape
    blk = min(rows, CAST_BLOCK_ELEMS // cols)
    assert rows % blk == 0 and blk % CHUNK == 0
    return pl.pallas_call(
        _cast_kernel,
        grid=(rows // blk,),
        in_specs=[pl.BlockSpec((None, blk, cols), lambda r: (layer, r, 0))],
        out_specs=pl.BlockSpec((None, blk, cols), lambda r: (0, r, 0)),
        out_shape=jax.ShapeDtypeStruct((1, rows, cols), BF16),
        compiler_params=_params(("parallel",)),
        name="cast_bf16",
    )(w)


def _ffn_kernel(x_ref, g_ref, sh_ref, sc_ref, gate_ref, w1_ref, w2_ref, *rest, final, nxt):
    fg_ref = rest[0] if final else None
    gn_ref, shn_ref, scn_ref = rest[:3] if nxt else (None, None, None)
    n_out = 2 if nxt else 1
    o_ref = rest[-2 - n_out]
    an_ref = rest[-3] if nxt else None
    a_ref, acc_ref = rest[-2:]
    f = pl.program_id(1)

    def mlp(a):
        h = jnp.dot(a, w1_ref[...], preferred_element_type=F32)
        h = jnp.maximum(h, 0.0)
        return jnp.dot((h * h).astype(BF16), w2_ref[...], preferred_element_type=F32)

    @pl.when(f == 0)
    def _():
        rows = x_ref.shape[0] // 2
        for r in range(2):
            sl = slice(r * rows, (r + 1) * rows)
            a = _norm_mod(x_ref[sl, :], g_ref[...], sc_ref[...], sh_ref[...]).astype(BF16)
            a_ref[sl, :] = a
            acc_ref[sl, :] = mlp(a)

    @pl.when(f > 0)
    def _():
        acc_ref[...] += mlp(a_ref[...])

    @pl.when(f == pl.num_programs(1) - 1)
    def _():
        out = x_ref[...] + gate_ref[...] * acc_ref[...]
        o_ref[...] = out if fg_ref is None else _rms(out) * fg_ref[...]
        if nxt:
            an_ref[...] = _norm_mod(out, gn_ref[...], scn_ref[...], shn_ref[...]).astype(an_ref.dtype)


def _ffn_call(lay, x, mod, g, w1, w2, layer, wl, n_rows, final_g=None, next_g=None):
    tm, d = TM_TOKEN, lay.d
    assert final_g is None or next_g is None
    extra_specs, extra = [], []
    if final_g is not None:
        extra_specs, extra = [pl.BlockSpec((1, d), lambda i, f: (0, 0))], [final_g]
    if next_g is not None:
        extra_specs = [pl.BlockSpec((None, 1, d), lambda i, f: (layer + 1, 0, 0)),
                       lay.mod_spec(layer + 1, 0, tm, 2), lay.mod_spec(layer + 1, 1, tm, 2)]
        extra = [next_g, mod, mod]
    x_spec = pl.BlockSpec((tm, d), lambda i, f: (i, 0))
    out_specs, out_shape = x_spec, jax.ShapeDtypeStruct((n_rows, d), F32)
    if next_g is not None:
        out_specs, out_shape = [x_spec, x_spec], [out_shape, jax.ShapeDtypeStruct((n_rows, d), BF16)]
    dff = w1.shape[2]
    tf = min(TF_FFN, dff)
    return pl.pallas_call(
        functools.partial(_ffn_kernel, final=final_g is not None, nxt=next_g is not None),
        grid=(n_rows // tm, dff // tf),
        in_specs=[x_spec,
                  pl.BlockSpec((None, 1, d), lambda i, f: (layer, 0, 0)),
                  lay.mod_spec(layer, 3, tm, 2), lay.mod_spec(layer, 4, tm, 2),
                  lay.mod_spec(layer, 5, tm, 2),
                  pl.BlockSpec((None, d, tf), lambda i, f: (wl, 0, f)),
                  pl.BlockSpec((None, tf, d), lambda i, f: (wl, f, 0))] + extra_specs,
        out_specs=out_specs,
        out_shape=out_shape,
        scratch_shapes=[pltpu.VMEM((tm, d), BF16), pltpu.VMEM((tm, d), F32)],
        compiler_params=_params(("parallel", "arbitrary")),
        name="ffn",
    )(x, g, mod, mod, mod, w1, w2, *extra)


def _cmul(ar, ai, br, bi):
    return ar * br - ai * bi, ar * bi + ai * br


def _tile_scan(sr, si, cr, ci, ap_ref, d, reverse):
    rows = lax.broadcasted_iota(jnp.int32, sr.shape, 0)

    def shift(v, k):
        if reverse:
            return jnp.where(rows < SUBLANES - k, pltpu.roll(v, SUBLANES - k, 0), 0.0)
        return jnp.where(rows >= k, pltpu.roll(v, k, 0), 0.0)

    pr, pi = sr, si
    for k in (1, 2, 4):
        akr = ap_ref[d, 0, k:k + 1, :]
        aki = ap_ref[d, 1, k:k + 1, :]
        tr, ti = _cmul(akr, aki, shift(pr, k), shift(pi, k))
        pr, pi = pr + tr, pi + ti
    base = SUBLANES if reverse else 0
    tab_r = ap_ref[d, 0, base:base + SUBLANES, :]
    tab_i = ap_ref[d, 1, base:base + SUBLANES, :]
    tr, ti = _cmul(tab_r, tab_i, cr, ci)
    hr, hi = shift(pr, 1) + tr, shift(pi, 1) + ti
    a8r = ap_ref[d, 0, 2 * SUBLANES:2 * SUBLANES + 1, :]
    a8i = ap_ref[d, 1, 2 * SUBLANES:2 * SUBLANES + 1, :]
    tr, ti = _cmul(a8r, a8i, cr, ci)
    last = 0 if reverse else SUBLANES - 1
    return hr, hi, pr[last:last + 1, :] + tr, pi[last:last + 1, :] + ti


def _split_bf16(v):
    hi = v.astype(BF16)
    return hi, (v - hi.astype(F32)).astype(BF16)


def _s5_core_kernel(*refs, nb, kl, kc):
    al_refs = refs[:nb]
    ac_refs = refs[nb:2 * nb]
    lamt_ref, bt_ref, laml_ref, ct_ref, ap_ref, dsk_ref, yl_ref, yc_ref = refs[2 * nb:2 * nb + 8]
    x_s, w_s, xc_s, wintra_s, bd_s, s_s, y_s = refs[2 * nb + 8:]
    gpl = LANES // CHUNK
    ns = x_s.shape[1] // LANES
    sw = s_s.shape[2]
    half = sw // 2
    lat_rows = nb * kl
    p_rows = ct_ref.shape[2]
    lane_group_p = lax.broadcasted_iota(jnp.int32, (p_rows, LANES), 1) // CHUNK

    for s in range(ns):
        for b in range(nb):
            x_s[b * kl:(b + 1) * kl, s * LANES:(s + 1) * LANES] = al_refs[b][s]
            x_s[lat_rows + b * kc:lat_rows + (b + 1) * kc, s * LANES:(s + 1) * LANES] = ac_refs[b][s]
    x = x_s[...]

    zero_tile = jnp.zeros((CHUNK, LANES), BF16)
    n_lt = half // LANES
    row_i = lax.broadcasted_iota(jnp.int32, (LANES, LANES), 0)
    lane_i = lax.broadcasted_iota(jnp.int32, (LANES, LANES), 1)
    same_group = row_i // CHUNK == lane_i // CHUNK
    even_lo = ((row_i // CHUNK) % 2 == 0) & (lane_i < LANES // 2)
    odd_hi = ((row_i // CHUNK) % 2 == 1) & (lane_i >= LANES // 2)
    k0 = jnp.where(row_i == lane_i, dsk_ref[...], 0.0)
    for d in range(2):
        cur, swapped = bt_ref[d, 0], bt_ref[d, 1]
        lam_a, lam_b = lamt_ref[d, 0], lamt_ref[d, 1]
        for e in range(ns):
            s = ns - 1 - e if d == 0 else e
            hi, lo = _split_bf16(cur)
            xc_s[0, s * LANES:(s + 1) * LANES, :] = hi
            xc_s[1, s * LANES:(s + 1) * LANES, :] = lo
            re_pos = jnp.where(even_lo, cur, jnp.where(odd_hi, swapped, 0.0)).astype(BF16)
            im_pos = jnp.where(even_lo, swapped, jnp.where(odd_hi, cur, 0.0)).astype(BF16)
            for g in range(gpl):
                r0 = g * CHUNK
                tiles = ([re_pos[r0:r0 + CHUNK, :] if lt == g // 2 else zero_tile for lt in range(n_lt)]
                         + [im_pos[r0:r0 + CHUNK, :] if lt == g // 2 else zero_tile for lt in range(n_lt)])
                w_s[d, s * LANES + r0:s * LANES + r0 + CHUNK, :] = jnp.concatenate(tiles, axis=-1)
            cur, swapped = lam_a * cur + lam_b * swapped, lam_a * swapped - lam_b * cur
        s_s[d] = jnp.dot(x, w_s[d], preferred_element_type=F32)
        c_hi, c_lo = _split_bf16(jnp.concatenate([ct_ref[d, 0], -ct_ref[d, 1]], axis=0))
        kall = (jnp.dot(xc_s[0], c_hi, preferred_element_type=F32)
                + jnp.dot(xc_s[0], c_lo, preferred_element_type=F32)
                + jnp.dot(xc_s[1], c_hi, preferred_element_type=F32))
        for e in range(ns):
            s = ns - 1 - e if d == 0 else e
            ktile = jnp.where(same_group, kall[s * LANES:(s + 1) * LANES, :], 0.0)
            if e == 0:
                k0 = k0 + ktile
            else:
                bd_s[ns - 1 + (e if d == 0 else -e)] = ktile.astype(BF16)
    bd_s[ns - 1] = k0.astype(BF16)

    n_ctx_tiles = kc // SUBLANES
    n_lat_tiles = kl // SUBLANES

    for s in range(ns):
        for t in range(ns):
            wintra_s[s * LANES:(s + 1) * LANES, t * LANES:(t + 1) * LANES] = bd_s[t - s + ns - 1]
    y_s[...] = jnp.dot(x, wintra_s[...], preferred_element_type=F32)

    carry = [jnp.zeros((1, half), F32)] * (4 * nb)
    for i in range(n_ctx_tiles + n_lat_tiles):
        for d in range(2):
            for b in range(nb):
                cr, ci = carry[2 * (d * nb + b)], carry[2 * (d * nb + b) + 1]
                if i < n_ctx_tiles:
                    row = lat_rows + b * kc + (i if d == 0 else n_ctx_tiles - 1 - i) * SUBLANES
                else:
                    il = i - n_ctx_tiles
                    row = b * kl + (il if d == 0 else n_lat_tiles - 1 - il) * SUBLANES
                sr = s_s[d, row:row + SUBLANES, 0:half]
                si = s_s[d, row:row + SUBLANES, half:sw]
                hr, hi, cr, ci = _tile_scan(sr, si, cr, ci, ap_ref, d, d == 1)
                s_s[d, row:row + SUBLANES, 0:half] = hr
                s_s[d, row:row + SUBLANES, half:sw] = hi
                carry[2 * (d * nb + b)], carry[2 * (d * nb + b) + 1] = cr, ci

    hcols = (ns * LANES) // 2
    for d in range(2):
        lam_r, lam_i = laml_ref[d, 0], laml_ref[d, 1]
        cur_r, cur_i = _cmul(lam_r, lam_i, ct_ref[d, 0], ct_ref[d, 1])
        for e in range(ns):
            t = e if d == 0 else ns - 1 - e
            c0 = t * LANES
            rbase = (c0 // hcols) * sw
            cc = c0 % hcols
            for part, ctile in ((0, cur_r), (1, -cur_i)):
                for g in range(gpl):
                    r0 = rbase + part * half + g * p_rows
                    w_s[d, r0:r0 + p_rows, cc:cc + LANES] = jnp.where(
                        lane_group_p == g, ctile, 0.0).astype(BF16)
            cur_r, cur_i = _cmul(lam_r, lam_i, cur_r, cur_i)
    for d in range(2):
        h = s_s[d].astype(BF16)
        for c in range(2):
            y_s[:, c * hcols:(c + 1) * hcols] += jnp.dot(
                h, w_s[d, c * sw:(c + 1) * sw, :], preferred_element_type=F32)

    for t in range(ns):
        for b in range(nb):
            yl_ref[b * ns + t] = y_s[b * kl:(b + 1) * kl, t * LANES:(t + 1) * LANES]
            yc_ref[b * ns + t] = y_s[lat_rows + b * kc:lat_rows + (b + 1) * kc, t * LANES:(t + 1) * LANES]


def _s5_core_call(lay, a, tabs, layer):
    lamt, bt, laml, ct, apow, dsk = tabs
    nb, d = lay.b, lay.d
    kl, kc = lay.l // CHUNK, lay.lc // CHUNK
    nj = d // LANES
    rows = nb * (kl + kc)
    sw = 2 * (LANES // CHUNK) * ct.shape[-2]
    assert 2 * sw == CHUNK * LANES
    a_l = a.reshape(lay.n // kl, kl, d)
    a_c = a.reshape(lay.n // kc, kc, d)
    ctx_blk0 = (nb * kl) // kc

    def tab_spec(t):
        return pl.BlockSpec((None, 2, t.shape[2], None) + t.shape[4:], lambda j: (layer, 0, 0, j, 0, 0))
    in_specs = ([pl.BlockSpec((CHUNK, kl, LANES), functools.partial(lambda j, b: (b, 0, j), b=b))
                 for b in range(nb)]
                + [pl.BlockSpec((CHUNK, kc, LANES), functools.partial(lambda j, b: (ctx_blk0 + b, 0, j), b=b))
                   for b in range(nb)]
                + [tab_spec(lamt), tab_spec(bt), tab_spec(laml), tab_spec(ct), tab_spec(apow),
                   pl.BlockSpec((None, None, 1, LANES), lambda j: (layer, j, 0, 0))])
    y_l, y_c = pl.pallas_call(
        functools.partial(_s5_core_kernel, nb=nb, kl=kl, kc=kc),
        grid=(nj,),
        in_specs=in_specs,
        out_specs=[pl.BlockSpec((nb * CHUNK, kl, LANES), lambda j: (0, 0, j)),
                   pl.BlockSpec((nb * CHUNK, kc, LANES), lambda j: (0, 0, j))],
        out_shape=[jax.ShapeDtypeStruct((nb * CHUNK, kl, d), F32),
                   jax.ShapeDtypeStruct((nb * CHUNK, kc, d), F32)],
        scratch_shapes=[pltpu.VMEM((rows, CHUNK * LANES), BF16),
                        pltpu.VMEM((2, CHUNK * LANES, sw), BF16),
                        pltpu.VMEM((2, CHUNK * LANES, LANES), BF16),
                        pltpu.VMEM((CHUNK * LANES, CHUNK * LANES), BF16),
                        pltpu.VMEM((2 * CHUNK - 1, LANES, LANES), BF16),
                        pltpu.VMEM((2, rows, sw), F32),
                        pltpu.VMEM((rows, CHUNK * LANES), F32)],
        compiler_params=_params(("parallel",)),
        name="s5_core",
    )(*([a_l] * nb + [a_c] * nb + [lamt, bt, laml, ct, apow, dsk]))
    return y_l.reshape(nb * lay.l, d), y_c.reshape(nb * lay.lc, d)


def _s5_tables(lam_re, lam_im, log_dt, b_re, b_im, c_re, c_im, d_skip):
    nl, _, ng, p = lam_re.shape
    grp = b_re.shape[-1]
    gpl = LANES // grp
    nj = ng // gpl
    dt = jnp.exp(log_dt.astype(F32))[..., None]
    lr, li = lam_re.astype(F32), lam_im.astype(F32)

    mag = jnp.exp(lr * dt)
    lbr, lbi = mag * jnp.cos(li * dt), mag * jnp.sin(li * dt)
    nr = lbr - 1.0
    den = lr * lr + li * li
    fr = (nr * lr + lbi * li) / den
    fi = (lbi * lr - nr * li) / den
    bbr = fr[..., None] * b_re - fi[..., None] * b_im
    bbi = fr[..., None] * b_im + fi[..., None] * b_re

    def rows_gc(re, im):
        v = jnp.concatenate([re, im], axis=-1)[:, :, :, None, :]
        return jnp.broadcast_to(v, (nl, 2, ng, grp, 2 * p)).reshape(nl, 2, nj, gpl * grp, 2 * p)

    def rows_p(v):
        v = v.reshape(nl, 2, nj, gpl, grp, p)
        return jnp.transpose(v, (0, 1, 2, 5, 3, 4)).reshape(nl, 2, nj, p, gpl * grp)

    lamt = jnp.stack([rows_gc(lbr, lbr), rows_gc(-lbi, lbi)], axis=2)
    bbr_t, bbi_t = jnp.swapaxes(bbr, -1, -2), jnp.swapaxes(bbi, -1, -2)
    bt = jnp.stack([jnp.concatenate([bbr_t, bbi_t], axis=-1), jnp.concatenate([bbi_t, bbr_t], axis=-1)], axis=2)
    bt = bt.reshape(nl, 2, 2, nj, gpl * grp, 2 * p)
    lam_bc = lambda v: jnp.broadcast_to(v[:, :, :, None, :], (nl, 2, ng, grp, p))
    laml = jnp.stack([rows_p(lam_bc(lbr)), rows_p(lam_bc(lbi))], axis=2)
    ct = jnp.stack([rows_p(c_re.astype(F32)), rows_p(c_im.astype(F32))], axis=2)

    m = (CHUNK * jnp.arange(SUBLANES + 1, dtype=F32))[:, None, None, None, None]
    amag = jnp.exp(lr * dt * m)

    def lay_a(v):
        tab = jnp.concatenate([v[:SUBLANES], v[:SUBLANES][::-1], v[SUBLANES:],
                               jnp.zeros((SUBLANES - 1,) + v.shape[1:], F32)], axis=0)
        tab = tab.reshape(3 * SUBLANES, nl, 2, nj, gpl * p)
        return jnp.transpose(tab, (1, 2, 3, 0, 4))
    apow = jnp.stack([lay_a(amag * jnp.cos(li * dt * m)), lay_a(amag * jnp.sin(li * dt * m))], axis=2)
    dsk = d_skip.astype(F32).reshape(nl, nj, 1, LANES)
    return lamt, bt, laml, ct, apow, dsk


def _glu_kernel(yl_ref, yc_ref, *rest, n_lat_tiles, split_x):
    if split_x:
        x = _pick(rest[0], rest[1], n_lat_tiles)
    else:
        x = rest[0][...]
    gate_ref, w_ref, b_ref, o_ref = rest[-4:]
    z = _gelu_tanh(_pick(yl_ref, yc_ref, n_lat_tiles))
    u = jnp.dot(z.astype(BF16), w_ref[...], preferred_element_type=F32) + b_ref[...]
    o_ref[...] = x + gate_ref[...] * (z * jax.nn.sigmoid(u))


def _glu_call(lay, y_l, y_c, x, mod, w, bias, layer, j):
    tm, d = TM_TOKEN, lay.d
    split_x = isinstance(x, tuple)
    x_specs = _split_specs(lay, tm) if split_x else [pl.BlockSpec((tm, d), lambda i: (i, 0))]
    return pl.pallas_call(
        functools.partial(_glu_kernel, n_lat_tiles=lay.n_lat // tm, split_x=split_x),
        grid=(lay.n // tm,),
        in_specs=_split_specs(lay, tm) + x_specs + [lay.mod_spec(layer, 2, tm),
                                                    pl.BlockSpec((None, d, d), lambda i: (j, 0, 0)),
                                                    pl.BlockSpec((None, 1, d), lambda i: (j, 0, 0))],
        out_specs=pl.BlockSpec((tm, d), lambda i: (i, 0)),
        out_shape=jax.ShapeDtypeStruct((lay.n, d), F32),
        compiler_params=_params(("parallel",)),
        name="s5_glu",
    )(y_l, y_c, *(x if split_x else (x,)), mod, w, bias)


def _rope(v, cos, sin):
    return v * cos + pltpu.roll(v, LANES // 2, 1) * sin


def _mla_proj_kernel(x_ref, g_ref, sh_ref, sc_ref, cos_ref, sin_ref, wdq_ref, qn_ref, wuq_ref,
                     wdkv_ref, kvn_ref, wukv_ref, q_ref, k_ref, v_ref, *, heads, scale):
    a = _norm_mod(x_ref[...], g_ref[...], sc_ref[...], sh_ref[...]).astype(BF16)
    cos, sin = cos_ref[...], sin_ref[...]
    hd = heads * LANES
    cq = jnp.dot(a, wdq_ref[...], preferred_element_type=F32)
    cq = (_rms(cq) * qn_ref[...]).astype(BF16)
    q = jnp.dot(cq, wuq_ref[...], preferred_element_type=F32)
    ckv = jnp.dot(a, wdkv_ref[...], preferred_element_type=F32)
    kvl = ckv.shape[1] - LANES
    ckvn = (_rms(ckv[:, :kvl]) * kvn_ref[...]).astype(BF16)
    kpe = _rope(ckv[:, kvl:], cos, sin).astype(BF16)
    kv = jnp.dot(ckvn, wukv_ref[...], preferred_element_type=F32)
    for h in range(heads):
        lo, hi_ = h * LANES, (h + 1) * LANES
        q_ref[h, :, 0:LANES] = (q[:, lo:hi_] * scale).astype(BF16)
        q_ref[h, :, LANES:2 * LANES] = (_rope(q[:, hd + lo:hd + hi_], cos, sin) * scale).astype(BF16)
        k_ref[h, :, 0:LANES] = kv[:, lo:hi_].astype(BF16)
        k_ref[h, :, LANES:2 * LANES] = kpe
        v_ref[h] = kv[:, hd + lo:hd + hi_].astype(BF16)


def _mla_proj_call(lay, x, mod, g, cos_t, sin_t, wts, layer, j, heads, scale):
    wdq, qn, wuq, wdkv, kvn, wukv = wts
    tm, d = TM_PROJ, lay.d
    full = lambda arr: pl.BlockSpec((None,) + arr.shape[1:], lambda i: (j, 0, 0),
                                    pipeline_mode=pl.Buffered(1))
    return pl.pallas_call(
        functools.partial(_mla_proj_kernel, heads=heads, scale=scale),
        grid=(lay.n // tm,),
        in_specs=[pl.BlockSpec((tm, d), lambda i: (i, 0)),
                  pl.BlockSpec((None, 1, d), lambda i: (layer, 0, 0)),
                  lay.mod_spec(layer, 0, tm), lay.mod_spec(layer, 1, tm),
                  pl.BlockSpec((tm, LANES), lambda i: (i, 0)),
                  pl.BlockSpec((tm, LANES), lambda i: (i, 0)),
                  full(wdq), full(qn), full(wuq), full(wdkv), full(kvn), full(wukv)],
        out_specs=[pl.BlockSpec((heads, tm, 2 * LANES), lambda i: (0, i, 0)),
                   pl.BlockSpec((heads, tm, 2 * LANES), lambda i: (0, i, 0)),
                   pl.BlockSpec((heads, tm, LANES), lambda i: (0, i, 0))],
        out_shape=[jax.ShapeDtypeStruct((heads, lay.n, 2 * LANES), BF16),
                   jax.ShapeDtypeStruct((heads, lay.n, 2 * LANES), BF16),
                   jax.ShapeDtypeStruct((heads, lay.n, LANES), BF16)],
        compiler_params=_params(("parallel",)),
        name="mla_proj",
    )(x, g, mod, mod, cos_t, sin_t, wdq, qn, wuq, wdkv, kvn, wukv)


def _attn_kernel(*refs, tk, n_kv, n_cast=0):
    q_ref, kc_ref, vc_ref = refs[:3]
    kl_ref, vl_ref = (refs[3], refs[4]) if n_kv else (None, None)
    if n_cast:
        for src, dst in zip(refs[5:5 + n_cast], refs[-n_cast:]):
            dst[...] = src[...].astype(dst.dtype)
    o_ref = refs[-1 - n_cast]
    nt = (((1,), (1,)), ((), ()))
    hps = q_ref.shape[0]
    qs, ms, ls, accs = [], [], [], []
    for h in range(hps):
        q = q_ref[h]
        s = lax.dot_general(q, kc_ref[h], nt, preferred_element_type=F32)
        m = jnp.max(s, axis=-1, keepdims=True)
        p = jnp.exp2(s - m)
        qs.append(q)
        ms.append(m)
        ls.append(jnp.sum(p, axis=-1, keepdims=True))
        accs.append(jnp.dot(p.astype(BF16), vc_ref[h], preferred_element_type=F32))
    for i in range(n_kv):
        for h in range(hps):
            s = lax.dot_general(qs[h], kl_ref[h, i * tk:(i + 1) * tk, :], nt, preferred_element_type=F32)
            m_new = jnp.maximum(ms[h], jnp.max(s, axis=-1, keepdims=True))
            alpha = jnp.exp2(ms[h] - m_new)
            p = jnp.exp2(s - m_new)
            ls[h] = alpha * ls[h] + jnp.sum(p, axis=-1, keepdims=True)
            accs[h] = alpha * accs[h] + jnp.dot(p.astype(BF16), vl_ref[h, i * tk:(i + 1) * tk, :],
                                                preferred_element_type=F32)
            ms[h] = m_new
    for h in range(hps):
        o_ref[:, h * LANES:(h + 1) * LANES] = (accs[h] / ls[h]).astype(o_ref.dtype)


def _attn_call(lay, q, k, v, heads, need_ctx, cast):
    nb, l, lc = lay.b, lay.l, lay.lc
    hps = ATTN_HEADS
    tq = min(TQ_ATTN, l)
    tk = min(TK_ATTN, l)
    nlq = l // tq
    nh = heads // hps
    ctx_blk0 = (nb * l) // lc
    ctx_specs = [pl.BlockSpec((hps, lc, 2 * LANES), lambda b, h, qi: (h, ctx_blk0 + b, 0)),
                 pl.BlockSpec((hps, lc, LANES), lambda b, h, qi: (h, ctx_blk0 + b, 0))]
    arrays, first, count = cast
    steps = nb * nh * nlq
    cast_in, cast_specs, cast_out_specs, cast_shapes = [], [], [], []
    for w in arrays:
        rows, cols = w.shape[1:]
        blk = count * rows // steps
        assert blk * steps == count * rows and blk % CHUNK == 0
        blk0 = first * rows // blk
        cast_in.append(w.reshape(-1, cols))
        cast_specs.append(pl.BlockSpec(
            (blk, cols), functools.partial(lambda b, h, qi, o: (o + (b * nh + h) * nlq + qi, 0), o=blk0)))
        cast_out_specs.append(pl.BlockSpec((blk, cols), lambda b, h, qi: ((b * nh + h) * nlq + qi, 0)))
        cast_shapes.append(jax.ShapeDtypeStruct((count * rows, cols), BF16))
    outs = pl.pallas_call(
        functools.partial(_attn_kernel, tk=tk, n_kv=l // tk, n_cast=len(arrays)),
        grid=(nb, nh, nlq),
        in_specs=[pl.BlockSpec((hps, tq, 2 * LANES), lambda b, h, qi: (h, b * nlq + qi, 0))] + ctx_specs
                 + [pl.BlockSpec((hps, l, 2 * LANES), lambda b, h, qi: (h, b, 0)),
                    pl.BlockSpec((hps, l, LANES), lambda b, h, qi: (h, b, 0))] + cast_specs,
        out_specs=[pl.BlockSpec((tq, hps * LANES), lambda b, h, qi: (b * nlq + qi, h))] + cast_out_specs,
        out_shape=[jax.ShapeDtypeStruct((lay.n_lat, heads * LANES), BF16)] + cast_shapes,
        compiler_params=_params(("parallel", "parallel", "arbitrary")),
        name="mla_attn",
    )(q, k, v, k, v, *cast_in)
    o_lat = outs[0]
    converted = [o.reshape((count,) + w.shape[1:]) for o, w in zip(outs[1:], arrays)]
    if not need_ctx:
        return o_lat, None, converted
    o_ctx = pl.pallas_call(
        functools.partial(_attn_kernel, tk=tk, n_kv=0),
        grid=(nb, heads // hps, 1),
        in_specs=[pl.BlockSpec((hps, lc, 2 * LANES), lambda b, h, qi: (h, ctx_blk0 + b, 0))] + ctx_specs,
        out_specs=pl.BlockSpec((lc, hps * LANES), lambda b, h, qi: (b, h)),
        out_shape=jax.ShapeDtypeStruct((nb * lc, heads * LANES), BF16),
        compiler_params=_params(("parallel", "parallel", "arbitrary")),
        name="mla_attn_ctx",
    )(q, k, v)
    return o_lat, o_ctx, converted


def _oproj_kernel(*refs, n_lat_tiles):
    ol_ref = refs[0]
    oc_ref = refs[1] if len(refs) == 6 else None
    x_ref, gate_ref, w_ref, out_ref = refs[-4:]
    o = ol_ref[...]
    if oc_ref is not None:
        o = jnp.where(pl.program_id(0) < n_lat_tiles, o, oc_ref[...])
    out_ref[...] = x_ref[...] + gate_ref[...] * jnp.dot(o, w_ref[...], preferred_element_type=F32)


def _oproj_call(lay, o_lat, o_ctx, x, mod, w, layer, j, n_rows):
    tm, d = TM_TOKEN, lay.d
    nlt = lay.n_lat // tm
    hd = o_lat.shape[1]
    o_specs = [pl.BlockSpec((tm, hd), lambda i: (jnp.minimum(i, nlt - 1), 0))]
    if o_ctx is not None:
        o_specs.append(pl.BlockSpec((tm, hd), lambda i: (jnp.maximum(i - nlt, 0), 0)))
    return pl.pallas_call(
        functools.partial(_oproj_kernel, n_lat_tiles=nlt),
        grid=(n_rows // tm,),
        in_specs=o_specs + [pl.BlockSpec((tm, d), lambda i: (i, 0)),
                            lay.mod_spec(layer, 2, tm),
                            pl.BlockSpec((None, hd, d), lambda i: (j, 0, 0))],
        out_specs=pl.BlockSpec((tm, d), lambda i: (i, 0)),
        out_shape=jax.ShapeDtypeStruct((n_rows, d), F32),
        compiler_params=_params(("parallel",)),
        name="mla_oproj",
    )(*([o_lat] + ([] if o_ctx is None else [o_ctx]) + [x, mod, w]))


def _to_chunk_major(t):
    b, n, d = t.shape
    return jnp.transpose(t.reshape(b, n // CHUNK, CHUNK, d), (0, 2, 1, 3)).reshape(b * n, d)


def _from_chunk_major(t, b):
    n, d = t.shape[0] // b, t.shape[1]
    return jnp.transpose(t.reshape(b, CHUNK, n // CHUNK, d), (0, 2, 1, 3)).reshape(b, n, d)


def _pad_rope_cols(w):
    nf = w.shape[-1] // 4
    w4 = w.reshape(w.shape[:-1] + (2, 2, nf))
    z = jnp.zeros(w.shape[:-1] + (LANES // 2 - 2 * nf,), w.dtype)
    x1 = w4[..., 0, :].reshape(w.shape[:-1] + (2 * nf,))
    x2 = w4[..., 1, :].reshape(w.shape[:-1] + (2 * nf,))
    return jnp.concatenate([x1, z, x2, z], axis=-1)


def _rope_tables(lay, rope_dim):
    nf = rope_dim // 4
    pos = jnp.arange(lay.l)
    inv = ROPE_THETA ** (-jnp.arange(nf, dtype=F32) / nf)
    row = (pos // GRID_W).astype(F32)
    col = (pos % GRID_W).astype(F32)
    ang = jnp.concatenate([row[:, None] * inv, col[:, None] * inv], axis=1)
    z = jnp.zeros((lay.l, LANES // 2 - 2 * nf), F32)
    cos_l = jnp.concatenate([jnp.cos(ang), z, jnp.cos(ang), z], axis=1)
    sin_l = jnp.concatenate([-jnp.sin(ang), z, jnp.sin(ang), z], axis=1)
    cos_l = _to_chunk_major(jnp.broadcast_to(cos_l[None], (lay.b, lay.l, LANES)))
    sin_l = _to_chunk_major(jnp.broadcast_to(sin_l[None], (lay.b, lay.l, LANES)))
    n_ctx = lay.b * lay.lc
    cos_t = jnp.concatenate([cos_l, jnp.ones((n_ctx, LANES), F32)], axis=0)
    sin_t = jnp.concatenate([sin_l, jnp.zeros((n_ctx, LANES), F32)], axis=0)
    return cos_t, sin_t


def _mla_weights(w_dq, q_norm, w_uq, w_dkv, kv_norm, w_ukv, heads, nope, rope, vdim):
    n, ql = w_uq.shape[:2]
    kvl = kv_norm.shape[1]
    uq = w_uq.reshape(n, ql, heads, nope + rope)
    uq_pe = _pad_rope_cols(uq[..., nope:]).reshape(n, ql, heads * LANES)
    wuq = jnp.concatenate([uq[..., :nope].reshape(n, ql, heads * nope), uq_pe], axis=-1)
    wdkv = jnp.concatenate([w_dkv[..., :kvl], _pad_rope_cols(w_dkv[..., kvl:])], axis=-1)
    ukv = w_ukv.reshape(n, kvl, heads, nope + vdim)
    wukv = jnp.concatenate([ukv[..., :nope].reshape(n, kvl, heads * nope),
                            ukv[..., nope:].reshape(n, kvl, heads * vdim)], axis=-1)
    return (w_dq.astype(BF16), q_norm.reshape(n, 1, ql).astype(F32), wuq.astype(BF16),
            wdkv.astype(BF16), kv_norm.reshape(n, 1, kvl).astype(F32), wukv.astype(BF16))


def kernel(x, c, ctx, c_ctx, ada_w, ada_b, norm1_g, norm2_g, final_norm_g, s5_lam_re, s5_lam_im, s5_log_dt, s5_b_re, s5_b_im, s5_c_re, s5_c_im, s5_d, s5_w_glu, s5_b_glu, mla_w_dq, mla_q_norm, mla_w_uq, mla_w_dkv, mla_kv_norm, mla_w_ukv, mla_w_o, ffn_w1, ffn_w2):
    b, l, d = x.shape
    lc = ctx.shape[1]
    depth = ada_w.shape[0]
    lay = _Layout(b, l, lc, d)

    kvl = mla_kv_norm.shape[1]
    rope = mla_w_dkv.shape[2] - kvl
    heads = (mla_w_uq.shape[2] + mla_w_o.shape[1] - mla_w_ukv.shape[2]) // rope
    vdim = mla_w_o.shape[1] // heads
    nope = mla_w_uq.shape[2] // heads - rope
    assert nope == LANES and vdim == LANES and 4 * (rope // 4) == rope and rope <= LANES
    assert b + 1 <= SUBLANES and s5_b_re.shape[-1] == CHUNK
    scale = (nope + rope) ** -0.5 * math.log2(math.e)

    xs = (_to_chunk_major(x), _to_chunk_major(ctx))

    cc = jnp.concatenate([c, c_ctx[None]], axis=0)
    mod = _ada_table(cc, ada_w, ada_b).reshape(depth, SUBLANES, 1, 6 * d)
    g1 = norm1_g.reshape(depth, 1, d)
    g2 = norm2_g.reshape(depth, 1, d)
    cos_t, sin_t = _rope_tables(lay, rope)

    tabs = _s5_tables(s5_lam_re, s5_lam_im, s5_log_dt, s5_b_re, s5_b_im, s5_c_re, s5_c_im, s5_d)
    wts = _mla_weights(mla_w_dq, mla_q_norm, mla_w_uq, mla_w_dkv, mla_kv_norm, mla_w_ukv,
                       heads, nope, rope, vdim)
    w_glu, b_glu = s5_w_glu.astype(BF16), s5_b_glu.reshape(-1, 1, d)
    w_o = mla_w_o.astype(BF16)
    mlp_w = {0: (_cast_layer_call(ffn_w1, 0), _cast_layer_call(ffn_w2, 0), 0)}

    a_next = None
    for i in range(depth):
        j = i // 2
        n_rows = lay.n_lat if i == depth - 1 else lay.n
        if i % 2 == 0:
            if a_next is None:
                assert i == 0
                a_next = _norm_call(lay, xs[0], xs[1], mod, g1, i)
            y_l, y_c = _s5_core_call(lay, a_next, tabs, j)
            xs = _glu_call(lay, y_l, y_c, xs, mod, w_glu, b_glu, i, j)
        else:
            q, k, v = _mla_proj_call(lay, xs, mod, g1, cos_t, sin_t, wts, i, j, heads, scale)
            count = min(2, depth - i)
            o_lat, o_ctx, (w1c, w2c) = _attn_call(lay, q, k, v, heads, n_rows == lay.n,
                                                  ([ffn_w1, ffn_w2], i, count))
            for t in range(count):
                mlp_w[i + t] = (w1c, w2c, t)
            xs = _oproj_call(lay, o_lat, o_ctx, xs, mod, w_o, i, j, n_rows)
        w1, w2, wl = mlp_w[i]
        final_g = final_norm_g.reshape(1, d) if i == depth - 1 else None
        feeds_s5 = i + 1 < depth and (i + 1) % 2 == 0
        res = _ffn_call(lay, xs, mod, g2, w1, w2, i, wl, n_rows, final_g, g1 if feeds_s5 else None)
        xs, a_next = res if feeds_s5 else (res, None)

    return _from_chunk_major(xs, b)
```

```python
import functools
import math

import jax
import jax.numpy as jnp
from jax import lax
from jax.experimental import pallas as pl
from jax.experimental.pallas import tpu as pltpu

F32 = jnp.float32
BF16 = jnp.bfloat16

EPS = 1e-6
GRID_W = 64
ROPE_THETA = 10000.0

LANES = 128
SUBLANES = 8
CHUNK = 16
VMEM_LIMIT = 56 * 1024 * 1024

TM_TOKEN = 512
TM_PROJ = 512
TQ_ATTN = 1024
ATTN_HEADS = 2
TK_ATTN = 256
TF_FFN = 1024
TN_ADA = 1536
CAST_BLOCK_ELEMS = 2 * 1024 * 1024


def _params(sem):
    return pltpu.CompilerParams(dimension_semantics=sem, vmem_limit_bytes=VMEM_LIMIT)


def _rms(x):
    return x * lax.rsqrt(jnp.mean(x * x, axis=-1, keepdims=True) + EPS)


def _norm_mod(x, g, sc, sh):
    return (_rms(x) * g) * (1.0 + sc) + sh


def _gelu_tanh(y):
    c = math.sqrt(2.0 / math.pi)
    return 0.5 * y * (1.0 + jnp.tanh(c * (y + 0.044715 * (y * y * y))))


def _ada_kernel(cb_ref, w_ref, b_ref, o_ref, sb_s):
    n_rows, d, _ = cb_ref.shape
    n_lt = w_ref.shape[1] // LANES
    @pl.when(pl.program_id(1) == 0)
    def _():
        cb = cb_ref[...]
        sb_s[...] = cb * jax.nn.sigmoid(cb)
    rows = lax.broadcasted_iota(jnp.int32, (SUBLANES, 1), 0)

    def body(kb, accs):
        k0 = pl.multiple_of(kb * SUBLANES, SUBLANES)
        w = w_ref[pl.ds(k0, SUBLANES), :]
        return tuple(acc + w * jnp.concatenate([sb_s[r, pl.ds(k0, SUBLANES), :]] * n_lt, axis=1)
                     for r, acc in enumerate(accs))
    accs = lax.fori_loop(0, d // SUBLANES, body, (jnp.zeros(o_ref.shape, F32),) * n_rows, unroll=4)
    out = jnp.broadcast_to(b_ref[...], o_ref.shape)
    for r, acc in enumerate(accs):
        out = out + jnp.where(rows == r, jnp.sum(acc, axis=0, keepdims=True), 0.0)
    o_ref[...] = out


def _ada_table(cc, ada_w, ada_b):
    depth, d, n6 = ada_w.shape
    n_rows = cc.shape[0]
    tn = min(TN_ADA, n6)
    cb = jnp.broadcast_to(cc[:, :, None], (n_rows, d, LANES))
    return pl.pallas_call(
        _ada_kernel,
        grid=(depth, n6 // tn),
        in_specs=[pl.BlockSpec((n_rows, d, LANES), lambda l, n: (0, 0, 0)),
                  pl.BlockSpec((None, d, tn), lambda l, n: (l, 0, n)),
                  pl.BlockSpec((None, 1, tn), lambda l, n: (l, 0, n))],
        out_specs=pl.BlockSpec((None, SUBLANES, tn), lambda l, n: (l, 0, n)),
        out_shape=jax.ShapeDtypeStruct((depth, SUBLANES, n6), F32),
        scratch_shapes=[pltpu.VMEM((n_rows, d, LANES), F32)],
        compiler_params=_params(("parallel", "arbitrary")),
        name="ada_table",
    )(cb, ada_w, ada_b.reshape(depth, 1, n6))


class _Layout:
    def __init__(self, b, l, lc, d):
        self.b, self.l, self.lc, self.d = b, l, lc, d
        self.n_lat = b * l
        self.n = b * (l + lc)

    def mod_row(self, tm):
        n_lat_tiles = self.n_lat // tm
        per_batch = self.l // tm
        ctx_row = self.b

        def row(i):
            return jnp.where(i < n_lat_tiles, i // per_batch, ctx_row)
        return row

    def mod_spec(self, layer, chunk, tm, ngrid=1):
        row = self.mod_row(tm)
        if ngrid == 1:
            return pl.BlockSpec((None, None, 1, self.d), lambda i: (layer, row(i), 0, chunk))
        return pl.BlockSpec((None, None, 1, self.d), lambda i, f: (layer, row(i), 0, chunk))


def _split_specs(lay, tm):
    nlt, d = lay.n_lat // tm, lay.d
    return [pl.BlockSpec((tm, d), lambda i: (jnp.minimum(i, nlt - 1), 0)),
            pl.BlockSpec((tm, d), lambda i: (jnp.maximum(i - nlt, 0), 0))]


def _pick(lat_ref, ctx_ref, n_lat_tiles):
    return jnp.where(pl.program_id(0) < n_lat_tiles, lat_ref[...], ctx_ref[...])


def _norm_kernel(xl_ref, xc_ref, g_ref, sh_ref, sc_ref, o_ref, *, n_lat_tiles):
    x = _pick(xl_ref, xc_ref, n_lat_tiles)
    o_ref[...] = _norm_mod(x, g_ref[...], sc_ref[...], sh_ref[...]).astype(o_ref.dtype)


def _norm_call(lay, x_lat, x_ctx, mod, g, layer):
    tm, d = TM_TOKEN, lay.d
    return pl.pallas_call(
        functools.partial(_norm_kernel, n_lat_tiles=lay.n_lat // tm),
        grid=(lay.n // tm,),
        in_specs=_split_specs(lay, tm) + [pl.BlockSpec((None, 1, d), lambda i: (layer, 0, 0)),
                                          lay.mod_spec(layer, 0, tm), lay.mod_spec(layer, 1, tm)],
        out_specs=pl.BlockSpec((tm, d), lambda i: (i, 0)),
        out_shape=jax.ShapeDtypeStruct((lay.n, d), BF16),
        compiler_params=_params(("parallel",)),
        name="norm_mod",
    )(x_lat, x_ctx, g, mod, mod)


def _cast_kernel(w_ref, o_ref):
    o_ref[...] = w_ref[...].astype(o_ref.dtype)


def _cast_layer_call(w, layer):
    _, rows, cols = w.shape
    blk = min(rows, CAST_BLOCK_ELEMS // cols)
    assert rows % blk == 0 and blk % CHUNK == 0
    return pl.pallas_call(
        _cast_kernel,
        grid=(rows // blk,),
        in_specs=[pl.BlockSpec((None, blk, cols), lambda r: (layer, r, 0))],
        out_specs=pl.BlockSpec((None, blk, cols), lambda r: (0, r, 0)),
        out_shape=jax.ShapeDtypeStruct((1, rows, cols), BF16),
        compiler_params=_params(("parallel",)),
        name="cast_bf16",
    )(w)


def _ffn_kernel(x_ref, g_ref, sh_ref, sc_ref, gate_ref, w1_ref, w2_ref, *rest, n_steps, final, nxt):
    fg_ref = rest[0] if final else None
    gn_ref, shn_ref, scn_ref = rest[:3] if nxt else (None, None, None)
    n_out = 2 if nxt else 1
    o_ref = rest[-2 - n_out]
    an_ref = rest[-3] if nxt else None
    a_ref, acc_ref = rest[-2:]
    f = pl.program_id(1)

    def mlp(a):
        h = jnp.dot(a, w1_ref[...], preferred_element_type=F32)
        h = jnp.maximum(h, 0.0)
        return jnp.dot((h * h).astype(BF16), w2_ref[...], preferred_element_type=F32)

    def norm_in(sl):
        return _norm_mod(x_ref[sl, :], g_ref[...], sc_ref[...], sh_ref[...]).astype(BF16)

    def finish(sl, acc):
        out = x_ref[sl, :] + gate_ref[...] * acc
        o_ref[sl, :] = out if fg_ref is None else _rms(out) * fg_ref[...]
        if nxt:
            an_ref[sl, :] = _norm_mod(out, gn_ref[...], scn_ref[...], shn_ref[...]).astype(an_ref.dtype)

    rows = x_ref.shape[0] // 2
    halves = [slice(r * rows, (r + 1) * rows) for r in range(2)]
    if n_steps == 1:
        for sl in halves:
            finish(sl, mlp(norm_in(sl)))
        return

    @pl.when(f == 0)
    def _():
        for sl in halves:
            a = norm_in(sl)
            a_ref[sl, :] = a
            acc_ref[sl, :] = mlp(a)

    @pl.when((f > 0) & (f < n_steps - 1))
    def _():
        acc_ref[...] += mlp(a_ref[...])

    @pl.when(f == n_steps - 1)
    def _():
        for sl in halves:
            finish(sl, acc_ref[sl, :] + mlp(a_ref[sl, :]))


def _ffn_call(lay, x, mod, g, w1, w2, layer, wl, n_rows, final_g=None, next_g=None):
    tm, d = TM_TOKEN, lay.d
    assert final_g is None or next_g is None
    extra_specs, extra = [], []
    if final_g is not None:
        extra_specs, extra = [pl.BlockSpec((1, d), lambda i, f: (0, 0))], [final_g]
    if next_g is not None:
        extra_specs = [pl.BlockSpec((None, 1, d), lambda i, f: (layer + 1, 0, 0)),
                       lay.mod_spec(layer + 1, 0, tm, 2), lay.mod_spec(layer + 1, 1, tm, 2)]
        extra = [next_g, mod, mod]
    x_spec = pl.BlockSpec((tm, d), lambda i, f: (i, 0))
    out_specs, out_shape = x_spec, jax.ShapeDtypeStruct((n_rows, d), F32)
    if next_g is not None:
        out_specs, out_shape = [x_spec, x_spec], [out_shape, jax.ShapeDtypeStruct((n_rows, d), BF16)]
    dff = w1.shape[2]
    tf = min(TF_FFN, dff)
    return pl.pallas_call(
        functools.partial(_ffn_kernel, n_steps=dff // tf, final=final_g is not None,
                          nxt=next_g is not None),
        grid=(n_rows // tm, dff // tf),
        in_specs=[x_spec,
                  pl.BlockSpec((None, 1, d), lambda i, f: (layer, 0, 0)),
                  lay.mod_spec(layer, 3, tm, 2), lay.mod_spec(layer, 4, tm, 2),
                  lay.mod_spec(layer, 5, tm, 2),
                  pl.BlockSpec((None, d, tf), lambda i, f: (wl, 0, f)),
                  pl.BlockSpec((None, tf, d), lambda i, f: (wl, f, 0))] + extra_specs,
        out_specs=out_specs,
        out_shape=out_shape,
        scratch_shapes=[pltpu.VMEM((tm, d), BF16), pltpu.VMEM((tm, d), F32)],
        compiler_params=_params(("parallel", "arbitrary")),
        name="ffn",
    )(x, g, mod, mod, mod, w1, w2, *extra)


def _cmul(ar, ai, br, bi):
    return ar * br - ai * bi, ar * bi + ai * br


def _tile_scan(sr, si, cr, ci, ap_ref, d, reverse):
    rows = lax.broadcasted_iota(jnp.int32, sr.shape, 0)

    def shift(v, k):
        if reverse:
            return jnp.where(rows < SUBLANES - k, pltpu.roll(v, SUBLANES - k, 0), 0.0)
        return jnp.where(rows >= k, pltpu.roll(v, k, 0), 0.0)

    pr, pi = sr, si
    for k in (1, 2, 4):
        akr = ap_ref[d, 0, k:k + 1, :]
        aki = ap_ref[d, 1, k:k + 1, :]
        tr, ti = _cmul(akr, aki, shift(pr, k), shift(pi, k))
        pr, pi = pr + tr, pi + ti
    base = SUBLANES if reverse else 0
    tab_r = ap_ref[d, 0, base:base + SUBLANES, :]
    tab_i = ap_ref[d, 1, base:base + SUBLANES, :]
    tr, ti = _cmul(tab_r, tab_i, cr, ci)
    hr, hi = shift(pr, 1) + tr, shift(pi, 1) + ti
    a8r = ap_ref[d, 0, 2 * SUBLANES:2 * SUBLANES + 1, :]
    a8i = ap_ref[d, 1, 2 * SUBLANES:2 * SUBLANES + 1, :]
    tr, ti = _cmul(a8r, a8i, cr, ci)
    last = 0 if reverse else SUBLANES - 1
    return hr, hi, pr[last:last + 1, :] + tr, pi[last:last + 1, :] + ti


def _split_bf16(v):
    hi = v.astype(BF16)
    return hi, (v - hi.astype(F32)).astype(BF16)


def _s5_core_kernel(*refs, nb, kl, kc):
    al_refs = refs[:nb]
    ac_refs = refs[nb:2 * nb]
    lamt_ref, bt_ref, laml_ref, ct_ref, ap_ref, dsk_ref, yl_ref, yc_ref = refs[2 * nb:2 * nb + 8]
    x_s, w_s, xc_s, wintra_s, bd_s, s_s, y_s = refs[2 * nb + 8:]
    gpl = LANES // CHUNK
    ns = x_s.shape[1] // LANES
    sw = s_s.shape[2]
    half = sw // 2
    lat_rows = nb * kl
    p_rows = ct_ref.shape[2]
    lane_group_p = lax.broadcasted_iota(jnp.int32, (p_rows, LANES), 1) // CHUNK

    for s in range(ns):
        for b in range(nb):
            x_s[b * kl:(b + 1) * kl, s * LANES:(s + 1) * LANES] = al_refs[b][s]
            x_s[lat_rows + b * kc:lat_rows + (b + 1) * kc, s * LANES:(s + 1) * LANES] = ac_refs[b][s]
    x = x_s[...]

    zero_tile = jnp.zeros((CHUNK, LANES), BF16)
    n_lt = half // LANES
    row_i = lax.broadcasted_iota(jnp.int32, (LANES, LANES), 0)
    lane_i = lax.broadcasted_iota(jnp.int32, (LANES, LANES), 1)
    same_group = row_i // CHUNK == lane_i // CHUNK
    even_lo = ((row_i // CHUNK) % 2 == 0) & (lane_i < LANES // 2)
    odd_hi = ((row_i // CHUNK) % 2 == 1) & (lane_i >= LANES // 2)
    k0 = jnp.where(row_i == lane_i, dsk_ref[...], 0.0)
    for d in range(2):
        cur, swapped = bt_ref[d, 0], bt_ref[d, 1]
        lam_a, lam_b = lamt_ref[d, 0], lamt_ref[d, 1]
        for e in range(ns):
            s = ns - 1 - e if d == 0 else e
            hi, lo = _split_bf16(cur)
            xc_s[s * LANES:(s + 1) * LANES, 0:LANES] = hi
            xc_s[s * LANES:(s + 1) * LANES, LANES:2 * LANES] = lo
            re_pos = jnp.where(even_lo, cur, jnp.where(odd_hi, swapped, 0.0)).astype(BF16)
            im_pos = jnp.where(even_lo, swapped, jnp.where(odd_hi, cur, 0.0)).astype(BF16)
            for g in range(gpl):
                r0 = g * CHUNK
                tiles = ([re_pos[r0:r0 + CHUNK, :] if lt == g // 2 else zero_tile for lt in range(n_lt)]
                         + [im_pos[r0:r0 + CHUNK, :] if lt == g // 2 else zero_tile for lt in range(n_lt)])
                w_s[d, s * LANES + r0:s * LANES + r0 + CHUNK, :] = jnp.concatenate(tiles, axis=-1)
            cur, swapped = lam_a * cur + lam_b * swapped, lam_a * swapped - lam_b * cur
        s_s[d] = jnp.dot(x, w_s[d], preferred_element_type=F32)
        c_hi, c_lo = _split_bf16(jnp.concatenate([ct_ref[d, 0], -ct_ref[d, 1]], axis=0))
        c_mat = jnp.concatenate([jnp.concatenate([c_hi, c_lo], axis=1),
                                 jnp.concatenate([c_hi, jnp.zeros_like(c_lo)], axis=1)], axis=0)
        kparts = jnp.dot(xc_s[...], c_mat, preferred_element_type=F32)
        kall = kparts[:, 0:LANES] + kparts[:, LANES:2 * LANES]
        for e in range(ns):
            s = ns - 1 - e if d == 0 else e
            ktile = jnp.where(same_group, kall[s * LANES:(s + 1) * LANES, :], 0.0)
            if e == 0:
                k0 = k0 + ktile
            else:
                bd_s[ns - 1 + (e if d == 0 else -e)] = ktile.astype(BF16)
    bd_s[ns - 1] = k0.astype(BF16)

    n_ctx_tiles = kc // SUBLANES
    n_lat_tiles = kl // SUBLANES

    for s in range(ns):
        for t in range(ns):
            wintra_s[s * LANES:(s + 1) * LANES, t * LANES:(t + 1) * LANES] = bd_s[t - s + ns - 1]
    y_s[...] = jnp.dot(x, wintra_s[...], preferred_element_type=F32)

    carry = [jnp.zeros((1, half), F32)] * (4 * nb)
    for i in range(n_ctx_tiles + n_lat_tiles):
        for d in range(2):
            for b in range(nb):
                cr, ci = carry[2 * (d * nb + b)], carry[2 * (d * nb + b) + 1]
                if i < n_ctx_tiles:
                    row = lat_rows + b * kc + (i if d == 0 else n_ctx_tiles - 1 - i) * SUBLANES
                else:
                    il = i - n_ctx_tiles
                    row = b * kl + (il if d == 0 else n_lat_tiles - 1 - il) * SUBLANES
                sr = s_s[d, row:row + SUBLANES, 0:half]
                si = s_s[d, row:row + SUBLANES, half:sw]
                hr, hi, cr, ci = _tile_scan(sr, si, cr, ci, ap_ref, d, d == 1)
                s_s[d, row:row + SUBLANES, 0:half] = hr
                s_s[d, row:row + SUBLANES, half:sw] = hi
                carry[2 * (d * nb + b)], carry[2 * (d * nb + b) + 1] = cr, ci

    hcols = (ns * LANES) // 2
    for d in range(2):
        lam_r, lam_i = laml_ref[d, 0], laml_ref[d, 1]
        cur_r, cur_i = _cmul(lam_r, lam_i, ct_ref[d, 0], ct_ref[d, 1])
        for e in range(ns):
            t = e if d == 0 else ns - 1 - e
            c0 = t * LANES
            rbase = (c0 // hcols) * sw
            cc = c0 % hcols
            for part, ctile in ((0, cur_r), (1, -cur_i)):
                for g in range(gpl):
                    r0 = rbase + part * half + g * p_rows
                    w_s[d, r0:r0 + p_rows, cc:cc + LANES] = jnp.where(
                        lane_group_p == g, ctile, 0.0).astype(BF16)
            cur_r, cur_i = _cmul(lam_r, lam_i, cur_r, cur_i)
    for d in range(2):
        h = s_s[d].astype(BF16)
        for c in range(2):
            y_s[:, c * hcols:(c + 1) * hcols] += jnp.dot(
                h, w_s[d, c * sw:(c + 1) * sw, :], preferred_element_type=F32)

    for t in range(ns):
        for b in range(nb):
            yl_ref[b * ns + t] = y_s[b * kl:(b + 1) * kl, t * LANES:(t + 1) * LANES]
            yc_ref[b * ns + t] = y_s[lat_rows + b * kc:lat_rows + (b + 1) * kc, t * LANES:(t + 1) * LANES]


def _s5_core_call(lay, a, tabs, layer):
    lamt, bt, laml, ct, apow, dsk = tabs
    nb, d = lay.b, lay.d
    kl, kc = lay.l // CHUNK, lay.lc // CHUNK
    nj = d // LANES
    rows = nb * (kl + kc)
    sw = 2 * (LANES // CHUNK) * ct.shape[-2]
    assert 2 * sw == CHUNK * LANES
    a_l = a.reshape(lay.n // kl, kl, d)
    a_c = a.reshape(lay.n // kc, kc, d)
    ctx_blk0 = (nb * kl) // kc

    def tab_spec(t):
        return pl.BlockSpec((None, 2, t.shape[2], None) + t.shape[4:], lambda j: (layer, 0, 0, j, 0, 0))
    in_specs = ([pl.BlockSpec((CHUNK, kl, LANES), functools.partial(lambda j, b: (b, 0, j), b=b))
                 for b in range(nb)]
                + [pl.BlockSpec((CHUNK, kc, LANES), functools.partial(lambda j, b: (ctx_blk0 + b, 0, j), b=b))
                   for b in range(nb)]
                + [tab_spec(lamt), tab_spec(bt), tab_spec(laml), tab_spec(ct), tab_spec(apow),
                   pl.BlockSpec((None, None, 1, LANES), lambda j: (layer, j, 0, 0))])
    y_l, y_c = pl.pallas_call(
        functools.partial(_s5_core_kernel, nb=nb, kl=kl, kc=kc),
        grid=(nj,),
        in_specs=in_specs,
        out_specs=[pl.BlockSpec((nb * CHUNK, kl, LANES), lambda j: (0, 0, j)),
                   pl.BlockSpec((nb * CHUNK, kc, LANES), lambda j: (0, 0, j))],
        out_shape=[jax.ShapeDtypeStruct((nb * CHUNK, kl, d), F32),
                   jax.ShapeDtypeStruct((nb * CHUNK, kc, d), F32)],
        scratch_shapes=[pltpu.VMEM((rows, CHUNK * LANES), BF16),
                        pltpu.VMEM((2, CHUNK * LANES, sw), BF16),
                        pltpu.VMEM((CHUNK * LANES, 2 * LANES), BF16),
                        pltpu.VMEM((CHUNK * LANES, CHUNK * LANES), BF16),
                        pltpu.VMEM((2 * CHUNK - 1, LANES, LANES), BF16),
                        pltpu.VMEM((2, rows, sw), F32),
                        pltpu.VMEM((rows, CHUNK * LANES), F32)],
        compiler_params=_params(("parallel",)),
        name="s5_core",
    )(*([a_l] * nb + [a_c] * nb + [lamt, bt, laml, ct, apow, dsk]))
    return y_l.reshape(nb * lay.l, d), y_c.reshape(nb * lay.lc, d)


def _s5_tables(lam_re, lam_im, log_dt, b_re, b_im, c_re, c_im, d_skip):
    nl, _, ng, p = lam_re.shape
    grp = b_re.shape[-1]
    gpl = LANES // grp
    nj = ng // gpl
    dt = jnp.exp(log_dt.astype(F32))[..., None]
    lr, li = lam_re.astype(F32), lam_im.astype(F32)

    mag = jnp.exp(lr * dt)
    lbr, lbi = mag * jnp.cos(li * dt), mag * jnp.sin(li * dt)
    nr = lbr - 1.0
    den = lr * lr + li * li
    fr = (nr * lr + lbi * li) / den
    fi = (lbi * lr - nr * li) / den
    bbr = fr[..., None] * b_re - fi[..., None] * b_im
    bbi = fr[..., None] * b_im + fi[..., None] * b_re

    def rows_gc(re, im):
        v = jnp.concatenate([re, im], axis=-1)[:, :, :, None, :]
        return jnp.broadcast_to(v, (nl, 2, ng, grp, 2 * p)).reshape(nl, 2, nj, gpl * grp, 2 * p)

    def rows_p(v):
        v = v.reshape(nl, 2, nj, gpl, grp, p)
        return jnp.transpose(v, (0, 1, 2, 5, 3, 4)).reshape(nl, 2, nj, p, gpl * grp)

    lamt = jnp.stack([rows_gc(lbr, lbr), rows_gc(-lbi, lbi)], axis=2)
    bbr_t, bbi_t = jnp.swapaxes(bbr, -1, -2), jnp.swapaxes(bbi, -1, -2)
    bt = jnp.stack([jnp.concatenate([bbr_t, bbi_t], axis=-1), jnp.concatenate([bbi_t, bbr_t], axis=-1)], axis=2)
    bt = bt.reshape(nl, 2, 2, nj, gpl * grp, 2 * p)
    lam_bc = lambda v: jnp.broadcast_to(v[:, :, :, None, :], (nl, 2, ng, grp, p))
    laml = jnp.stack([rows_p(lam_bc(lbr)), rows_p(lam_bc(lbi))], axis=2)
    ct = jnp.stack([rows_p(c_re.astype(F32)), rows_p(c_im.astype(F32))], axis=2)

    m = (CHUNK * jnp.arange(SUBLANES + 1, dtype=F32))[:, None, None, None, None]
    amag = jnp.exp(lr * dt * m)

    def lay_a(v):
        tab = jnp.concatenate([v[:SUBLANES], v[:SUBLANES][::-1], v[SUBLANES:],
                               jnp.zeros((SUBLANES - 1,) + v.shape[1:], F32)], axis=0)
        tab = tab.reshape(3 * SUBLANES, nl, 2, nj, gpl * p)
        return jnp.transpose(tab, (1, 2, 3, 0, 4))
    apow = jnp.stack([lay_a(amag * jnp.cos(li * dt * m)), lay_a(amag * jnp.sin(li * dt * m))], axis=2)
    dsk = d_skip.astype(F32).reshape(nl, nj, 1, LANES)
    return lamt, bt, laml, ct, apow, dsk


def _glu_kernel(yl_ref, yc_ref, *rest, n_lat_tiles, split_x):
    if split_x:
        x = _pick(rest[0], rest[1], n_lat_tiles)
    else:
        x = rest[0][...]
    gate_ref, w_ref, b_ref, o_ref = rest[-4:]
    z = _gelu_tanh(_pick(yl_ref, yc_ref, n_lat_tiles))
    u = jnp.dot(z.astype(BF16), w_ref[...], preferred_element_type=F32) + b_ref[...]
    o_ref[...] = x + gate_ref[...] * (z * jax.nn.sigmoid(u))


def _glu_call(lay, y_l, y_c, x, mod, w, bias, layer, j):
    tm, d = TM_TOKEN, lay.d
    split_x = isinstance(x, tuple)
    x_specs = _split_specs(lay, tm) if split_x else [pl.BlockSpec((tm, d), lambda i: (i, 0))]
    return pl.pallas_call(
        functools.partial(_glu_kernel, n_lat_tiles=lay.n_lat // tm, split_x=split_x),
        grid=(lay.n // tm,),
        in_specs=_split_specs(lay, tm) + x_specs + [lay.mod_spec(layer, 2, tm),
                                                    pl.BlockSpec((None, d, d), lambda i: (j, 0, 0)),
                                                    pl.BlockSpec((None, 1, d), lambda i: (j, 0, 0))],
        out_specs=pl.BlockSpec((tm, d), lambda i: (i, 0)),
        out_shape=jax.ShapeDtypeStruct((lay.n, d), F32),
        compiler_params=_params(("parallel",)),
        name="s5_glu",
    )(y_l, y_c, *(x if split_x else (x,)), mod, w, bias)


def _rope(v, cos, sin):
    return v * cos + pltpu.roll(v, LANES // 2, 1) * sin


def _mla_proj_kernel(x_ref, g_ref, sh_ref, sc_ref, cos_ref, sin_ref, wdq_ref, qn_ref, wuq_ref,
                     wdkv_ref, kvn_ref, wukv_ref, q_ref, k_ref, v_ref, *, heads, scale):
    a = _norm_mod(x_ref[...], g_ref[...], sc_ref[...], sh_ref[...]).astype(BF16)
    cos, sin = cos_ref[...], sin_ref[...]
    hd = heads * LANES
    cq = jnp.dot(a, wdq_ref[...], preferred_element_type=F32)
    cq = (_rms(cq) * qn_ref[...]).astype(BF16)
    q = jnp.dot(cq, wuq_ref[...], preferred_element_type=F32)
    ckv = jnp.dot(a, wdkv_ref[...], preferred_element_type=F32)
    kvl = ckv.shape[1] - LANES
    ckvn = (_rms(ckv[:, :kvl]) * kvn_ref[...]).astype(BF16)
    kpe = _rope(ckv[:, kvl:], cos, sin).astype(BF16)
    kv = jnp.dot(ckvn, wukv_ref[...], preferred_element_type=F32)
    for h in range(heads):
        lo, hi_ = h * LANES, (h + 1) * LANES
        q_ref[h, :, 0:LANES] = (q[:, lo:hi_] * scale).astype(BF16)
        q_ref[h, :, LANES:2 * LANES] = (_rope(q[:, hd + lo:hd + hi_], cos, sin) * scale).astype(BF16)
        k_ref[h, :, 0:LANES] = kv[:, lo:hi_].astype(BF16)
        k_ref[h, :, LANES:2 * LANES] = kpe
        v_ref[h] = kv[:, hd + lo:hd + hi_].astype(BF16)


def _mla_proj_call(lay, x, mod, g, cos_t, sin_t, wts, layer, j, heads, scale):
    wdq, qn, wuq, wdkv, kvn, wukv = wts
    tm, d = TM_PROJ, lay.d
    full = lambda arr: pl.BlockSpec((None,) + arr.shape[1:], lambda i: (j, 0, 0),
                                    pipeline_mode=pl.Buffered(1))
    return pl.pallas_call(
        functools.partial(_mla_proj_kernel, heads=heads, scale=scale),
        grid=(lay.n // tm,),
        in_specs=[pl.BlockSpec((tm, d), lambda i: (i, 0)),
                  pl.BlockSpec((None, 1, d), lambda i: (layer, 0, 0)),
                  lay.mod_spec(layer, 0, tm), lay.mod_spec(layer, 1, tm),
                  pl.BlockSpec((tm, LANES), lambda i: (i, 0)),
                  pl.BlockSpec((tm, LANES), lambda i: (i, 0)),
                  full(wdq), full(qn), full(wuq), full(wdkv), full(kvn), full(wukv)],
        out_specs=[pl.BlockSpec((heads, tm, 2 * LANES), lambda i: (0, i, 0)),
                   pl.BlockSpec((heads, tm, 2 * LANES), lambda i: (0, i, 0)),
                   pl.BlockSpec((heads, tm, LANES), lambda i: (0, i, 0))],
        out_shape=[jax.ShapeDtypeStruct((heads, lay.n, 2 * LANES), BF16),
                   jax.ShapeDtypeStruct((heads, lay.n, 2 * LANES), BF16),
                   jax.ShapeDtypeStruct((heads, lay.n, LANES), BF16)],
        compiler_params=_params(("parallel",)),
        name="mla_proj",
    )(x, g, mod, mod, cos_t, sin_t, wdq, qn, wuq, wdkv, kvn, wukv)


def _attn_kernel(*refs, tk, n_kv, n_cast=0):
    q_ref, kc_ref, vc_ref = refs[:3]
    kl_ref, vl_ref = (refs[3], refs[4]) if n_kv else (None, None)
    if n_cast:
        for src, dst in zip(refs[5:5 + n_cast], refs[-n_cast:]):
            dst[...] = src[...].astype(dst.dtype)
    o_ref = refs[-1 - n_cast]
    nt = (((1,), (1,)), ((), ()))
    hps = q_ref.shape[0]
    qs, ms, ls, accs = [], [], [], []
    for h in range(hps):
        q = q_ref[h]
        s = lax.dot_general(q, kc_ref[h], nt, preferred_element_type=F32)
        m = jnp.max(s, axis=-1, keepdims=True)
        p = jnp.exp2(s - m)
        qs.append(q)
        ms.append(m)
        ls.append(jnp.sum(p, axis=-1, keepdims=True))
        accs.append(jnp.dot(p.astype(BF16), vc_ref[h], preferred_element_type=F32))
    for i in range(n_kv):
        for h in range(hps):
            s = lax.dot_general(qs[h], kl_ref[h, i * tk:(i + 1) * tk, :], nt, preferred_element_type=F32)
            m_new = jnp.maximum(ms[h], jnp.max(s, axis=-1, keepdims=True))
            alpha = jnp.exp2(ms[h] - m_new)
            p = jnp.exp2(s - m_new)
            ls[h] = alpha * ls[h] + jnp.sum(p, axis=-1, keepdims=True)
            accs[h] = alpha * accs[h] + jnp.dot(p.astype(BF16), vl_ref[h, i * tk:(i + 1) * tk, :],
                                                preferred_element_type=F32)
            ms[h] = m_new
    for h in range(hps):
        o_ref[:, h * LANES:(h + 1) * LANES] = (accs[h] / ls[h]).astype(o_ref.dtype)


def _attn_call(lay, q, k, v, heads, need_ctx, cast):
    nb, l, lc = lay.b, lay.l, lay.lc
    hps = ATTN_HEADS
    tq = min(TQ_ATTN, l)
    tk = min(TK_ATTN, l)
    nlq = l // tq
    nh = heads // hps
    ctx_blk0 = (nb * l) // lc
    ctx_specs = [pl.BlockSpec((hps, lc, 2 * LANES), lambda b, h, qi: (h, ctx_blk0 + b, 0)),
                 pl.BlockSpec((hps, lc, LANES), lambda b, h, qi: (h, ctx_blk0 + b, 0))]
    arrays, first, count = cast
    steps = nb * nh * nlq
    cast_in, cast_specs, cast_out_specs, cast_shapes = [], [], [], []
    for w in arrays:
        rows, cols = w.shape[1:]
        blk = count * rows // steps
        assert blk * steps == count * rows and blk % CHUNK == 0
        blk0 = first * rows // blk
        cast_in.append(w.reshape(-1, cols))
        cast_specs.append(pl.BlockSpec(
            (blk, cols), functools.partial(lambda b, h, qi, o: (o + (b * nh + h) * nlq + qi, 0), o=blk0)))
        cast_out_specs.append(pl.BlockSpec((blk, cols), lambda b, h, qi: ((b * nh + h) * nlq + qi, 0)))
        cast_shapes.append(jax.ShapeDtypeStruct((count * rows, cols), BF16))
    outs = pl.pallas_call(
        functools.partial(_attn_kernel, tk=tk, n_kv=l // tk, n_cast=len(arrays)),
        grid=(nb, nh, nlq),
        in_specs=[pl.BlockSpec((hps, tq, 2 * LANES), lambda b, h, qi: (h, b * nlq + qi, 0))] + ctx_specs
                 + [pl.BlockSpec((hps, l, 2 * LANES), lambda b, h, qi: (h, b, 0)),
                    pl.BlockSpec((hps, l, LANES), lambda b, h, qi: (h, b, 0))] + cast_specs,
        out_specs=[pl.BlockSpec((tq, hps * LANES), lambda b, h, qi: (b * nlq + qi, h))] + cast_out_specs,
        out_shape=[jax.ShapeDtypeStruct((lay.n_lat, heads * LANES), BF16)] + cast_shapes,
        compiler_params=_params(("parallel", "parallel", "arbitrary")),
        name="mla_attn",
    )(q, k, v, k, v, *cast_in)
    o_lat = outs[0]
    converted = [o.reshape((count,) + w.shape[1:]) for o, w in zip(outs[1:], arrays)]
    if not need_ctx:
        return o_lat, None, converted
    o_ctx = pl.pallas_call(
        functools.partial(_attn_kernel, tk=tk, n_kv=0),
        grid=(nb, heads // hps, 1),
        in_specs=[pl.BlockSpec((hps, lc, 2 * LANES), lambda b, h, qi: (h, ctx_blk0 + b, 0))] + ctx_specs,
        out_specs=pl.BlockSpec((lc, hps * LANES), lambda b, h, qi: (b, h)),
        out_shape=jax.ShapeDtypeStruct((nb * lc, heads * LANES), BF16),
        compiler_params=_params(("parallel", "parallel", "arbitrary")),
        name="mla_attn_ctx",
    )(q, k, v)
    return o_lat, o_ctx, converted


def _oproj_kernel(*refs, n_lat_tiles):
    ol_ref = refs[0]
    oc_ref = refs[1] if len(refs) == 6 else None
    x_ref, gate_ref, w_ref, out_ref = refs[-4:]
    o = ol_ref[...]
    if oc_ref is not None:
        o = jnp.where(pl.program_id(0) < n_lat_tiles, o, oc_ref[...])
    out_ref[...] = x_ref[...] + gate_ref[...] * jnp.dot(o, w_ref[...], preferred_element_type=F32)


def _oproj_call(lay, o_lat, o_ctx, x, mod, w, layer, j, n_rows):
    tm, d = TM_TOKEN, lay.d
    nlt = lay.n_lat // tm
    hd = o_lat.shape[1]
    o_specs = [pl.BlockSpec((tm, hd), lambda i: (jnp.minimum(i, nlt - 1), 0))]
    if o_ctx is not None:
        o_specs.append(pl.BlockSpec((tm, hd), lambda i: (jnp.maximum(i - nlt, 0), 0)))
    return pl.pallas_call(
        functools.partial(_oproj_kernel, n_lat_tiles=nlt),
        grid=(n_rows // tm,),
        in_specs=o_specs + [pl.BlockSpec((tm, d), lambda i: (i, 0)),
                            lay.mod_spec(layer, 2, tm),
                            pl.BlockSpec((None, hd, d), lambda i: (j, 0, 0))],
        out_specs=pl.BlockSpec((tm, d), lambda i: (i, 0)),
        out_shape=jax.ShapeDtypeStruct((n_rows, d), F32),
        compiler_params=_params(("parallel",)),
        name="mla_oproj",
    )(*([o_lat] + ([] if o_ctx is None else [o_ctx]) + [x, mod, w]))


def _to_chunk_major(t):
    b, n, d = t.shape
    return jnp.transpose(t.reshape(b, n // CHUNK, CHUNK, d), (0, 2, 1, 3)).reshape(b * n, d)


def _from_chunk_major(t, b):
    n, d = t.shape[0] // b, t.shape[1]
    return jnp.transpose(t.reshape(b, CHUNK, n // CHUNK, d), (0, 2, 1, 3)).reshape(b, n, d)


def _pad_rope_cols(w):
    nf = w.shape[-1] // 4
    w4 = w.reshape(w.shape[:-1] + (2, 2, nf))
    z = jnp.zeros(w.shape[:-1] + (LANES // 2 - 2 * nf,), w.dtype)
    x1 = w4[..., 0, :].reshape(w.shape[:-1] + (2 * nf,))
    x2 = w4[..., 1, :].reshape(w.shape[:-1] + (2 * nf,))
    return jnp.concatenate([x1, z, x2, z], axis=-1)


def _rope_tables(lay, rope_dim):
    nf = rope_dim // 4
    pos = jnp.arange(lay.l)
    inv = ROPE_THETA ** (-jnp.arange(nf, dtype=F32) / nf)
    row = (pos // GRID_W).astype(F32)
    col = (pos % GRID_W).astype(F32)
    ang = jnp.concatenate([row[:, None] * inv, col[:, None] * inv], axis=1)
    z = jnp.zeros((lay.l, LANES // 2 - 2 * nf), F32)
    cos_l = jnp.concatenate([jnp.cos(ang), z, jnp.cos(ang), z], axis=1)
    sin_l = jnp.concatenate([-jnp.sin(ang), z, jnp.sin(ang), z], axis=1)
    cos_l = _to_chunk_major(jnp.broadcast_to(cos_l[None], (lay.b, lay.l, LANES)))
    sin_l = _to_chunk_major(jnp.broadcast_to(sin_l[None], (lay.b, lay.l, LANES)))
    n_ctx = lay.b * lay.lc
    cos_t = jnp.concatenate([cos_l, jnp.ones((n_ctx, LANES), F32)], axis=0)
    sin_t = jnp.concatenate([sin_l, jnp.zeros((n_ctx, LANES), F32)], axis=0)
    return cos_t, sin_t


def _mla_weights(w_dq, q_norm, w_uq, w_dkv, kv_norm, w_ukv, heads, nope, rope, vdim):
    n, ql = w_uq.shape[:2]
    kvl = kv_norm.shape[1]
    uq = w_uq.reshape(n, ql, heads, nope + rope)
    uq_pe = _pad_rope_cols(uq[..., nope:]).reshape(n, ql, heads * LANES)
    wuq = jnp.concatenate([uq[..., :nope].reshape(n, ql, heads * nope), uq_pe], axis=-1)
    wdkv = jnp.concatenate([w_dkv[..., :kvl], _pad_rope_cols(w_dkv[..., kvl:])], axis=-1)
    ukv = w_ukv.reshape(n, kvl, heads, nope + vdim)
    wukv = jnp.concatenate([ukv[..., :nope].reshape(n, kvl, heads * nope),
                            ukv[..., nope:].reshape(n, kvl, heads * vdim)], axis=-1)
    return (w_dq.astype(BF16), q_norm.reshape(n, 1, ql).astype(F32), wuq.astype(BF16),
            wdkv.astype(BF16), kv_norm.reshape(n, 1, kvl).astype(F32), wukv.astype(BF16))


def kernel(x, c, ctx, c_ctx, ada_w, ada_b, norm1_g, norm2_g, final_norm_g, s5_lam_re, s5_lam_im, s5_log_dt, s5_b_re, s5_b_im, s5_c_re, s5_c_im, s5_d, s5_w_glu, s5_b_glu, mla_w_dq, mla_q_norm, mla_w_uq, mla_w_dkv, mla_kv_norm, mla_w_ukv, mla_w_o, ffn_w1, ffn_w2):
    b, l, d = x.shape
    lc = ctx.shape[1]
    depth = ada_w.shape[0]
    lay = _Layout(b, l, lc, d)

    kvl = mla_kv_norm.shape[1]
    rope = mla_w_dkv.shape[2] - kvl
    heads = (mla_w_uq.shape[2] + mla_w_o.shape[1] - mla_w_ukv.shape[2]) // rope
    vdim = mla_w_o.shape[1] // heads
    nope = mla_w_uq.shape[2] // heads - rope
    assert nope == LANES and vdim == LANES and 4 * (rope // 4) == rope and rope <= LANES
    assert b + 1 <= SUBLANES and s5_b_re.shape[-1] == CHUNK
    scale = (nope + rope) ** -0.5 * math.log2(math.e)

    xs = (_to_chunk_major(x), _to_chunk_major(ctx))

    cc = jnp.concatenate([c, c_ctx[None]], axis=0)
    mod = _ada_table(cc, ada_w, ada_b).reshape(depth, SUBLANES, 1, 6 * d)
    g1 = norm1_g.reshape(depth, 1, d)
    g2 = norm2_g.reshape(depth, 1, d)
    cos_t, sin_t = _rope_tables(lay, rope)

    tabs = _s5_tables(s5_lam_re, s5_lam_im, s5_log_dt, s5_b_re, s5_b_im, s5_c_re, s5_c_im, s5_d)
    wts = _mla_weights(mla_w_dq, mla_q_norm, mla_w_uq, mla_w_dkv, mla_kv_norm, mla_w_ukv,
                       heads, nope, rope, vdim)
    w_glu, b_glu = s5_w_glu.astype(BF16), s5_b_glu.reshape(-1, 1, d)
    w_o = mla_w_o.astype(BF16)
    mlp_w = {0: (_cast_layer_call(ffn_w1, 0), _cast_layer_call(ffn_w2, 0), 0)}

    a_next = None
    for i in range(depth):
        j = i // 2
        n_rows = lay.n_lat if i == depth - 1 else lay.n
        if i % 2 == 0:
            if a_next is None:
                assert i == 0
                a_next = _norm_call(lay, xs[0], xs[1], mod, g1, i)
            y_l, y_c = _s5_core_call(lay, a_next, tabs, j)
            xs = _glu_call(lay, y_l, y_c, xs, mod, w_glu, b_glu, i, j)
        else:
            q, k, v = _mla_proj_call(lay, xs, mod, g1, cos_t, sin_t, wts, i, j, heads, scale)
            count = min(2, depth - i)
            o_lat, o_ctx, (w1c, w2c) = _attn_call(lay, q, k, v, heads, n_rows == lay.n,
                                                  ([ffn_w1, ffn_w2], i, count))
            for t in range(count):
                mlp_w[i + t] = (w1c, w2c, t)
            xs = _oproj_call(lay, o_lat, o_ctx, xs, mod, w_o, i, j, n_rows)
        w1, w2, wl = mlp_w[i]
        final_g = final_norm_g.reshape(1, d) if i == depth - 1 else None
        feeds_s5 = i + 1 < depth and (i + 1) % 2 == 0
        res = _ffn_call(lay, xs, mod, g2, w1, w2, i, wl, n_rows, final_g, g1 if feeds_s5 else None)
        xs, a_next = res if feeds_s5 else (res, None)

    return _from_chunk_major(xs, b)
```

```python
import functools
import math

import jax
import jax.numpy as jnp
from jax import lax
from jax.experimental import pallas as pl
from jax.experimental.pallas import tpu as pltpu

F32 = jnp.float32
BF16 = jnp.bfloat16

EPS = 1e-6
GRID_W = 64
ROPE_THETA = 10000.0

LANES = 128
SUBLANES = 8
CHUNK = 16
VMEM_LIMIT = 56 * 1024 * 1024

TM_TOKEN = 512
TM_PROJ = 512
TQ_ATTN = 1024
ATTN_HEADS = 2
TK_ATTN = 256
TF_FFN = 1024
TN_ADA = 1536
CAST_BLOCK_ELEMS = 2 * 1024 * 1024


def _params(sem):
    return pltpu.CompilerParams(dimension_semantics=sem, vmem_limit_bytes=VMEM_LIMIT)


def _rms(x):
    return x * lax.rsqrt(jnp.mean(x * x, axis=-1, keepdims=True) + EPS)


def _norm_mod(x, g, sc, sh):
    return (_rms(x) * g) * (1.0 + sc) + sh


def _gelu_tanh(y):
    c = math.sqrt(2.0 / math.pi)
    return 0.5 * y * (1.0 + jnp.tanh(c * (y + 0.044715 * (y * y * y))))


def _ada_kernel(cb_ref, w_ref, b_ref, o_ref, sb_s):
    n_rows, d, _ = cb_ref.shape
    n_lt = w_ref.shape[1] // LANES
    @pl.when(pl.program_id(1) == 0)
    def _():
        cb = cb_ref[...]
        sb_s[...] = cb * jax.nn.sigmoid(cb)
    rows = lax.broadcasted_iota(jnp.int32, (SUBLANES, 1), 0)

    def body(kb, accs):
        k0 = pl.multiple_of(kb * SUBLANES, SUBLANES)
        w = w_ref[pl.ds(k0, SUBLANES), :]
        return tuple(acc + w * jnp.concatenate([sb_s[r, pl.ds(k0, SUBLANES), :]] * n_lt, axis=1)
                     for r, acc in enumerate(accs))
    accs = lax.fori_loop(0, d // SUBLANES, body, (jnp.zeros(o_ref.shape, F32),) * n_rows, unroll=4)
    out = jnp.broadcast_to(b_ref[...], o_ref.shape)
    for r, acc in enumerate(accs):
        out = out + jnp.where(rows == r, jnp.sum(acc, axis=0, keepdims=True), 0.0)
    o_ref[...] = out


def _ada_table(cc, ada_w, ada_b):
    depth, d, n6 = ada_w.shape
    n_rows = cc.shape[0]
    tn = min(TN_ADA, n6)
    cb = jnp.broadcast_to(cc[:, :, None], (n_rows, d, LANES))
    return pl.pallas_call(
        _ada_kernel,
        grid=(depth, n6 // tn),
        in_specs=[pl.BlockSpec((n_rows, d, LANES), lambda l, n: (0, 0, 0)),
                  pl.BlockSpec((None, d, tn), lambda l, n: (l, 0, n)),
                  pl.BlockSpec((None, 1, tn), lambda l, n: (l, 0, n))],
        out_specs=pl.BlockSpec((None, SUBLANES, tn), lambda l, n: (l, 0, n)),
        out_shape=jax.ShapeDtypeStruct((depth, SUBLANES, n6), F32),
        scratch_shapes=[pltpu.VMEM((n_rows, d, LANES), F32)],
        compiler_params=_params(("parallel", "arbitrary")),
        name="ada_table",
    )(cb, ada_w, ada_b.reshape(depth, 1, n6))


class _Layout:
    def __init__(self, b, l, lc, d):
        self.b, self.l, self.lc, self.d = b, l, lc, d
        self.n_lat = b * l
        self.n = b * (l + lc)

    def mod_row(self, tm):
        n_lat_tiles = self.n_lat // tm
        per_batch = self.l // tm
        ctx_row = self.b

        def row(i):
            return jnp.where(i < n_lat_tiles, i // per_batch, ctx_row)
        return row

    def mod_spec(self, layer, chunk, tm, ngrid=1):
        row = self.mod_row(tm)
        if ngrid == 1:
            return pl.BlockSpec((None, None, 1, self.d), lambda i: (layer, row(i), 0, chunk))
        return pl.BlockSpec((None, None, 1, self.d), lambda i, f: (layer, row(i), 0, chunk))


def _split_specs(lay, tm):
    nlt, d = lay.n_lat // tm, lay.d
    return [pl.BlockSpec((tm, d), lambda i: (jnp.minimum(i, nlt - 1), 0)),
            pl.BlockSpec((tm, d), lambda i: (jnp.maximum(i - nlt, 0), 0))]


def _pick(lat_ref, ctx_ref, n_lat_tiles):
    return jnp.where(pl.program_id(0) < n_lat_tiles, lat_ref[...], ctx_ref[...])


def _norm_kernel(xl_ref, xc_ref, g_ref, sh_ref, sc_ref, o_ref, *, n_lat_tiles):
    x = _pick(xl_ref, xc_ref, n_lat_tiles)
    o_ref[...] = _norm_mod(x, g_ref[...], sc_ref[...], sh_ref[...]).astype(o_ref.dtype)


def _norm_call(lay, x_lat, x_ctx, mod, g, layer):
    tm, d = TM_TOKEN, lay.d
    return pl.pallas_call(
        functools.partial(_norm_kernel, n_lat_tiles=lay.n_lat // tm),
        grid=(lay.n // tm,),
        in_specs=_split_specs(lay, tm) + [pl.BlockSpec((None, 1, d), lambda i: (layer, 0, 0)),
                                          lay.mod_spec(layer, 0, tm), lay.mod_spec(layer, 1, tm)],
        out_specs=pl.BlockSpec((tm, d), lambda i: (i, 0)),
        out_shape=jax.ShapeDtypeStruct((lay.n, d), BF16),
        compiler_params=_params(("parallel",)),
        name="norm_mod",
    )(x_lat, x_ctx, g, mod, mod)


def _cast_kernel(w_ref, o_ref):
    o_ref[...] = w_ref[...].astype(o_ref.dtype)


def _cast_layer_call(w, layer):
    _, rows, cols = w.shape
    blk = min(rows, CAST_BLOCK_ELEMS // cols)
    assert rows % blk == 0 and blk % CHUNK == 0
    return pl.pallas_call(
        _cast_kernel,
        grid=(rows // blk,),
        in_specs=[pl.BlockSpec((None, blk, cols), lambda r: (layer, r, 0))],
        out_specs=pl.BlockSpec((None, blk, cols), lambda r: (0, r, 0)),
        out_shape=jax.ShapeDtypeStruct((1, rows, cols), BF16),
        compiler_params=_params(("parallel",)),
        name="cast_bf16",
    )(w)


def _ffn_kernel(x_ref, g_ref, sh_ref, sc_ref, gate_ref, w1_ref, w2_ref, *rest, n_steps, final, nxt):
    fg_ref = rest[0] if final else None
    gn_ref, shn_ref, scn_ref = rest[:3] if nxt else (None, None, None)
    n_out = 2 if nxt else 1
    o_ref = rest[-2 - n_out]
    an_ref = rest[-3] if nxt else None
    a_ref, acc_ref = rest[-2:]
    f = pl.program_id(1)

    def mlp(a):
        h = jnp.dot(a, w1_ref[...], preferred_element_type=F32)
        h = jnp.maximum(h, 0.0)
        return jnp.dot((h * h).astype(BF16), w2_ref[...], preferred_element_type=F32)

    def norm_in(sl):
        return _norm_mod(x_ref[sl, :], g_ref[...], sc_ref[...], sh_ref[...]).astype(BF16)

    def finish(sl, acc):
        out = x_ref[sl, :] + gate_ref[...] * acc
        o_ref[sl, :] = out if fg_ref is None else _rms(out) * fg_ref[...]
        if nxt:
            an_ref[sl, :] = _norm_mod(out, gn_ref[...], scn_ref[...], shn_ref[...]).astype(an_ref.dtype)

    rows = x_ref.shape[0] // 2
    halves = [slice(r * rows, (r + 1) * rows) for r in range(2)]
    if n_steps == 1:
        for sl in halves:
            finish(sl, mlp(norm_in(sl)))
        return

    @pl.when(f == 0)
    def _():
        for sl in halves:
            a = norm_in(sl)
            a_ref[sl, :] = a
            acc_ref[sl, :] = mlp(a)

    @pl.when((f > 0) & (f < n_steps - 1))
    def _():
        acc_ref[...] += mlp(a_ref[...])

    @pl.when(f == n_steps - 1)
    def _():
        for sl in halves:
            finish(sl, acc_ref[sl, :] + mlp(a_ref[sl, :]))


def _ffn_call(lay, x, mod, g, w1, w2, layer, wl, n_rows, final_g=None, next_g=None):
    tm, d = TM_TOKEN, lay.d
    assert final_g is None or next_g is None
    extra_specs, extra = [], []
    if final_g is not None:
        extra_specs, extra = [pl.BlockSpec((1, d), lambda i, f: (0, 0))], [final_g]
    if next_g is not None:
        extra_specs = [pl.BlockSpec((None, 1, d), lambda i, f: (layer + 1, 0, 0)),
                       lay.mod_spec(layer + 1, 0, tm, 2), lay.mod_spec(layer + 1, 1, tm, 2)]
        extra = [next_g, mod, mod]
    x_spec = pl.BlockSpec((tm, d), lambda i, f: (i, 0))
    out_specs, out_shape = x_spec, jax.ShapeDtypeStruct((n_rows, d), F32)
    if next_g is not None:
        out_specs, out_shape = [x_spec, x_spec], [out_shape, jax.ShapeDtypeStruct((n_rows, d), BF16)]
    dff = w1.shape[2]
    tf = min(TF_FFN, dff)
    return pl.pallas_call(
        functools.partial(_ffn_kernel, n_steps=dff // tf, final=final_g is not None,
                          nxt=next_g is not None),
        grid=(n_rows // tm, dff // tf),
        in_specs=[x_spec,
                  pl.BlockSpec((None, 1, d), lambda i, f: (layer, 0, 0)),
                  lay.mod_spec(layer, 3, tm, 2), lay.mod_spec(layer, 4, tm, 2),
                  lay.mod_spec(layer, 5, tm, 2),
                  pl.BlockSpec((None, d, tf), lambda i, f: (wl, 0, f)),
                  pl.BlockSpec((None, tf, d), lambda i, f: (wl, f, 0))] + extra_specs,
        out_specs=out_specs,
        out_shape=out_shape,
        scratch_shapes=[pltpu.VMEM((tm, d), BF16), pltpu.VMEM((tm, d), F32)],
        compiler_params=_params(("parallel", "arbitrary")),
        name="ffn",
    )(x, g, mod, mod, mod, w1, w2, *extra)


def _cmul(ar, ai, br, bi):
    return ar * br - ai * bi, ar * bi + ai * br


def _tile_scan(sr, si, cr, ci, ap_ref, d, reverse):
    rows = lax.broadcasted_iota(jnp.int32, sr.shape, 0)

    def shift(v, k):
        if reverse:
            return jnp.where(rows < SUBLANES - k, pltpu.roll(v, SUBLANES - k, 0), 0.0)
        return jnp.where(rows >= k, pltpu.roll(v, k, 0), 0.0)

    pr, pi = sr, si
    for k in (1, 2, 4):
        akr = ap_ref[d, 0, k:k + 1, :]
        aki = ap_ref[d, 1, k:k + 1, :]
        tr, ti = _cmul(akr, aki, shift(pr, k), shift(pi, k))
        pr, pi = pr + tr, pi + ti
    base = SUBLANES if reverse else 0
    tab_r = ap_ref[d, 0, base:base + SUBLANES, :]
    tab_i = ap_ref[d, 1, base:base + SUBLANES, :]
    tr, ti = _cmul(tab_r, tab_i, cr, ci)
    hr, hi = shift(pr, 1) + tr, shift(pi, 1) + ti
    a8r = ap_ref[d, 0, 2 * SUBLANES:2 * SUBLANES + 1, :]
    a8i = ap_ref[d, 1, 2 * SUBLANES:2 * SUBLANES + 1, :]
    tr, ti = _cmul(a8r, a8i, cr, ci)
    last = 0 if reverse else SUBLANES - 1
    return hr, hi, pr[last:last + 1, :] + tr, pi[last:last + 1, :] + ti


def _split_bf16(v):
    hi = v.astype(BF16)
    return hi, (v - hi.astype(F32)).astype(BF16)


def _s5_core_kernel(*refs, nb, kl, kc):
    al_refs = refs[:nb]
    ac_refs = refs[nb:2 * nb]
    lamt_ref, bt_ref, laml_ref, ct_ref, ap_ref, dsk_ref, yl_ref, yc_ref = refs[2 * nb:2 * nb + 8]
    x_s, x2_s, w_s, xc_s, wintra_s, bd_s, s_s, y_s = refs[2 * nb + 8:]
    gpl = LANES // CHUNK
    ns = x_s.shape[1] // LANES
    sw = s_s.shape[2]
    half = sw // 2
    lat_rows = nb * kl
    p_rows = ct_ref.shape[2]

    for s in range(ns):
        for b in range(nb):
            x_s[b * kl:(b + 1) * kl, s * LANES:(s + 1) * LANES] = al_refs[b][s]
            x_s[lat_rows + b * kc:lat_rows + (b + 1) * kc, s * LANES:(s + 1) * LANES] = ac_refs[b][s]
    hl = LANES // 2
    lane_x = lax.broadcasted_iota(jnp.int32, (x_s.shape[0], LANES), 1)
    for pair in range(ns // 2):
        a = x_s[:, (2 * pair) * LANES:(2 * pair + 1) * LANES].astype(F32)
        b2 = x_s[:, (2 * pair + 1) * LANES:(2 * pair + 2) * LANES].astype(F32)
        lo_half = jnp.where(lane_x < hl, a, pltpu.roll(b2, hl, 1))
        hi_half = jnp.where(lane_x < hl, pltpu.roll(a, hl, 1), b2)
        x2_s[0, :, pair * LANES:(pair + 1) * LANES] = lo_half.astype(BF16)
        x2_s[1, :, pair * LANES:(pair + 1) * LANES] = hi_half.astype(BF16)

    zero_tile = jnp.zeros((CHUNK, LANES), BF16)
    n_lt = half // LANES
    gph = gpl // 2
    hrows = (ns // 2) * LANES
    hw = half // 2
    row_i = lax.broadcasted_iota(jnp.int32, (LANES, LANES), 0)
    lane_i = lax.broadcasted_iota(jnp.int32, (LANES, LANES), 1)
    same_group = row_i // CHUNK == lane_i // CHUNK
    even_lo = ((row_i // CHUNK) % 2 == 0) & (lane_i < LANES // 2)
    odd_hi = ((row_i // CHUNK) % 2 == 1) & (lane_i >= LANES // 2)
    k0 = jnp.where(row_i == lane_i, dsk_ref[...], 0.0)
    for d in range(2):
        cur, swapped = bt_ref[d, 0], bt_ref[d, 1]
        lam_a, lam_b = lamt_ref[d, 0], lamt_ref[d, 1]
        for e in range(ns):
            s = ns - 1 - e if d == 0 else e
            hi, lo = _split_bf16(cur)
            xc_s[s * LANES:(s + 1) * LANES, 0:LANES] = hi
            xc_s[s * LANES:(s + 1) * LANES, LANES:2 * LANES] = lo
            re_pos = jnp.where(even_lo, cur, jnp.where(odd_hi, swapped, 0.0)).astype(BF16)
            im_pos = jnp.where(even_lo, swapped, jnp.where(odd_hi, cur, 0.0)).astype(BF16)
            for g in range(gpl):
                r0 = g * CHUNK
                h, gl = g // gph, g % gph
                tiles = ([re_pos[r0:r0 + CHUNK, :] if lt == gl // 2 else zero_tile for lt in range(n_lt // 2)]
                         + [im_pos[r0:r0 + CHUNK, :] if lt == gl // 2 else zero_tile for lt in range(n_lt // 2)])
                row = h * hrows + (s // 2) * LANES + (s % 2) * hl + gl * CHUNK
                w_s[d, row:row + CHUNK, 0:2 * hw] = jnp.concatenate(tiles, axis=-1)
            cur, swapped = lam_a * cur + lam_b * swapped, lam_a * swapped - lam_b * cur
        for h in range(2):
            sh = jnp.dot(x2_s[h], w_s[d, h * hrows:(h + 1) * hrows, 0:2 * hw], preferred_element_type=F32)
            s_s[d, :, h * hw:(h + 1) * hw] = sh[:, 0:hw]
            s_s[d, :, half + h * hw:half + (h + 1) * hw] = sh[:, hw:2 * hw]
        c_hi, c_lo = _split_bf16(jnp.concatenate([ct_ref[d, 0], -ct_ref[d, 1]], axis=0))
        c_mat = jnp.concatenate([jnp.concatenate([c_hi, c_lo], axis=1),
                                 jnp.concatenate([c_hi, jnp.zeros_like(c_lo)], axis=1)], axis=0)
        kparts = jnp.dot(xc_s[...], c_mat, preferred_element_type=F32)
        kall = kparts[:, 0:LANES] + kparts[:, LANES:2 * LANES]
        for e in range(ns):
            s = ns - 1 - e if d == 0 else e
            ktile = jnp.where(same_group, kall[s * LANES:(s + 1) * LANES, :], 0.0)
            if e == 0:
                k0 = k0 + ktile
            else:
                bd_s[ns - 1 + (e if d == 0 else -e)] = ktile
    bd_s[ns - 1] = k0

    n_ctx_tiles = kc // SUBLANES
    n_lat_tiles = kl // SUBLANES

    lane_h = lax.broadcasted_iota(jnp.int32, (hl, LANES), 1)
    n_pair = ns // 2
    for h in range(2):
        def blk(lag, h=h):
            return bd_s[lag + ns - 1, h * hl:(h + 1) * hl, :]
        for lag in range(2 - ns, ns - 1, 2):
            if h == 0:
                top = jnp.where(lane_h < hl, blk(lag), pltpu.roll(blk(lag + 1), hl, 1))
                bot = jnp.where(lane_h < hl, blk(lag - 1), pltpu.roll(blk(lag), hl, 1))
            else:
                top = jnp.where(lane_h < hl, pltpu.roll(blk(lag), hl, 1), blk(lag + 1))
                bot = jnp.where(lane_h < hl, pltpu.roll(blk(lag - 1), hl, 1), blk(lag))
            tile = jnp.concatenate([top, bot], axis=0).astype(BF16)
            for ps in range(n_pair):
                pt = ps + lag // 2
                if 0 <= pt < n_pair:
                    wintra_s[h, ps * LANES:(ps + 1) * LANES, pt * LANES:(pt + 1) * LANES] = tile
        y_s[h] = jnp.dot(x2_s[h], wintra_s[h], preferred_element_type=F32)

    carry = [jnp.zeros((1, half), F32)] * (4 * nb)
    for i in range(n_ctx_tiles + n_lat_tiles):
        for d in range(2):
            for b in range(nb):
                cr, ci = carry[2 * (d * nb + b)], carry[2 * (d * nb + b) + 1]
                if i < n_ctx_tiles:
                    row = lat_rows + b * kc + (i if d == 0 else n_ctx_tiles - 1 - i) * SUBLANES
                else:
                    il = i - n_ctx_tiles
                    row = b * kl + (il if d == 0 else n_lat_tiles - 1 - il) * SUBLANES
                sr = s_s[d, row:row + SUBLANES, 0:half]
                si = s_s[d, row:row + SUBLANES, half:sw]
                hr, hi, cr, ci = _tile_scan(sr, si, cr, ci, ap_ref, d, d == 1)
                s_s[d, row:row + SUBLANES, 0:half] = hr
                s_s[d, row:row + SUBLANES, half:sw] = hi
                carry[2 * (d * nb + b)], carry[2 * (d * nb + b) + 1] = cr, ci

    lane_p = lax.broadcasted_iota(jnp.int32, (p_rows, LANES), 1)
    hrows2 = 2 * hw
    for d in range(2):
        lam_r, lam_i = laml_ref[d, 0], laml_ref[d, 1]
        cur_r, cur_i = _cmul(lam_r, lam_i, ct_ref[d, 0], ct_ref[d, 1])
        by_t = [None] * ns
        for e in range(ns):
            by_t[e if d == 0 else ns - 1 - e] = (cur_r, -cur_i)
            cur_r, cur_i = _cmul(lam_r, lam_i, cur_r, cur_i)
        for pair in range(ns // 2):
            for part in range(2):
                a, b2 = by_t[2 * pair][part], by_t[2 * pair + 1][part]
                both = (jnp.where(lane_p < hl, a, pltpu.roll(b2, hl, 1)),
                        jnp.where(lane_p < hl, pltpu.roll(a, hl, 1), b2))
                for h in range(2):
                    for gl in range(gph):
                        r0 = h * hrows2 + part * hw + gl * p_rows
                        w_s[d, r0:r0 + p_rows, pair * LANES:(pair + 1) * LANES] = jnp.where(
                            (lane_p % hl) // CHUNK == gl, both[h], 0.0).astype(BF16)
    for h in range(2):
        for d in range(2):
            hin = jnp.concatenate([s_s[d, :, h * hw:(h + 1) * hw],
                                   s_s[d, :, half + h * hw:half + (h + 1) * hw]], axis=1).astype(BF16)
            y_s[h] += jnp.dot(hin, w_s[d, h * hrows2:(h + 1) * hrows2, :], preferred_element_type=F32)

    lane_y = lax.broadcasted_iota(jnp.int32, (y_s.shape[1], LANES), 1)
    for pair in range(n_pair):
        y0 = y_s[0, :, pair * LANES:(pair + 1) * LANES]
        y1 = y_s[1, :, pair * LANES:(pair + 1) * LANES]
        tiles = (jnp.where(lane_y < hl, y0, pltpu.roll(y1, hl, 1)),
                 jnp.where(lane_y < hl, pltpu.roll(y0, hl, 1), y1))
        for which, tile in enumerate(tiles):
            t = 2 * pair + which
            for b in range(nb):
                yl_ref[b * ns + t] = tile[b * kl:(b + 1) * kl, :]
                yc_ref[b * ns + t] = tile[lat_rows + b * kc:lat_rows + (b + 1) * kc, :]


def _s5_core_call(lay, a, tabs, layer):
    lamt, bt, laml, ct, apow, dsk = tabs
    nb, d = lay.b, lay.d
    kl, kc = lay.l // CHUNK, lay.lc // CHUNK
    nj = d // LANES
    rows = nb * (kl + kc)
    sw = 2 * (LANES // CHUNK) * ct.shape[-2]
    assert 2 * sw == CHUNK * LANES
    a_l = a.reshape(lay.n // kl, kl, d)
    a_c = a.reshape(lay.n // kc, kc, d)
    ctx_blk0 = (nb * kl) // kc

    def tab_spec(t):
        return pl.BlockSpec((None, 2, t.shape[2], None) + t.shape[4:], lambda j: (layer, 0, 0, j, 0, 0))
    in_specs = ([pl.BlockSpec((CHUNK, kl, LANES), functools.partial(lambda j, b: (b, 0, j), b=b))
                 for b in range(nb)]
                + [pl.BlockSpec((CHUNK, kc, LANES), functools.partial(lambda j, b: (ctx_blk0 + b, 0, j), b=b))
                   for b in range(nb)]
                + [tab_spec(lamt), tab_spec(bt), tab_spec(laml), tab_spec(ct), tab_spec(apow),
                   pl.BlockSpec((None, None, 1, LANES), lambda j: (layer, j, 0, 0))])
    y_l, y_c = pl.pallas_call(
        functools.partial(_s5_core_kernel, nb=nb, kl=kl, kc=kc),
        grid=(nj,),
        in_specs=in_specs,
        out_specs=[pl.BlockSpec((nb * CHUNK, kl, LANES), lambda j: (0, 0, j)),
                   pl.BlockSpec((nb * CHUNK, kc, LANES), lambda j: (0, 0, j))],
        out_shape=[jax.ShapeDtypeStruct((nb * CHUNK, kl, d), F32),
                   jax.ShapeDtypeStruct((nb * CHUNK, kc, d), F32)],
        scratch_shapes=[pltpu.VMEM((rows, CHUNK * LANES), BF16),
                        pltpu.VMEM((2, rows, CHUNK * LANES // 2), BF16),
                        pltpu.VMEM((2, CHUNK * LANES, sw), BF16),
                        pltpu.VMEM((CHUNK * LANES, 2 * LANES), BF16),
                        pltpu.VMEM((2, CHUNK * LANES // 2, CHUNK * LANES // 2), BF16),
                        pltpu.VMEM((2 * CHUNK - 1, LANES, LANES), F32),
                        pltpu.VMEM((2, rows, sw), F32),
                        pltpu.VMEM((2, rows, CHUNK * LANES // 2), F32)],
        compiler_params=_params(("parallel",)),
        name="s5_core",
    )(*([a_l] * nb + [a_c] * nb + [lamt, bt, laml, ct, apow, dsk]))
    return y_l.reshape(nb * lay.l, d), y_c.reshape(nb * lay.lc, d)


def _s5_tables(lam_re, lam_im, log_dt, b_re, b_im, c_re, c_im, d_skip):
    nl, _, ng, p = lam_re.shape
    grp = b_re.shape[-1]
    gpl = LANES // grp
    nj = ng // gpl
    dt = jnp.exp(log_dt.astype(F32))[..., None]
    lr, li = lam_re.astype(F32), lam_im.astype(F32)

    mag = jnp.exp(lr * dt)
    lbr, lbi = mag * jnp.cos(li * dt), mag * jnp.sin(li * dt)
    nr = lbr - 1.0
    den = lr * lr + li * li
    fr = (nr * lr + lbi * li) / den
    fi = (lbi * lr - nr * li) / den
    bbr = fr[..., None] * b_re - fi[..., None] * b_im
    bbi = fr[..., None] * b_im + fi[..., None] * b_re

    def rows_gc(re, im):
        v = jnp.concatenate([re, im], axis=-1)[:, :, :, None, :]
        return jnp.broadcast_to(v, (nl, 2, ng, grp, 2 * p)).reshape(nl, 2, nj, gpl * grp, 2 * p)

    def rows_p(v):
        v = v.reshape(nl, 2, nj, gpl, grp, p)
        return jnp.transpose(v, (0, 1, 2, 5, 3, 4)).reshape(nl, 2, nj, p, gpl * grp)

    lamt = jnp.stack([rows_gc(lbr, lbr), rows_gc(-lbi, lbi)], axis=2)
    bbr_t, bbi_t = jnp.swapaxes(bbr, -1, -2), jnp.swapaxes(bbi, -1, -2)
    bt = jnp.stack([jnp.concatenate([bbr_t, bbi_t], axis=-1), jnp.concatenate([bbi_t, bbr_t], axis=-1)], axis=2)
    bt = bt.reshape(nl, 2, 2, nj, gpl * grp, 2 * p)
    lam_bc = lambda v: jnp.broadcast_to(v[:, :, :, None, :], (nl, 2, ng, grp, p))
    laml = jnp.stack([rows_p(lam_bc(lbr)), rows_p(lam_bc(lbi))], axis=2)
    ct = jnp.stack([rows_p(c_re.astype(F32)), rows_p(c_im.astype(F32))], axis=2)

    m = (CHUNK * jnp.arange(SUBLANES + 1, dtype=F32))[:, None, None, None, None]
    amag = jnp.exp(lr * dt * m)

    def lay_a(v):
        tab = jnp.concatenate([v[:SUBLANES], v[:SUBLANES][::-1], v[SUBLANES:],
                               jnp.zeros((SUBLANES - 1,) + v.shape[1:], F32)], axis=0)
        tab = tab.reshape(3 * SUBLANES, nl, 2, nj, gpl * p)
        return jnp.transpose(tab, (1, 2, 3, 0, 4))
    apow = jnp.stack([lay_a(amag * jnp.cos(li * dt * m)), lay_a(amag * jnp.sin(li * dt * m))], axis=2)
    dsk = d_skip.astype(F32).reshape(nl, nj, 1, LANES)
    return lamt, bt, laml, ct, apow, dsk


def _glu_kernel(yl_ref, yc_ref, *rest, n_lat_tiles, split_x):
    if split_x:
        x = _pick(rest[0], rest[1], n_lat_tiles)
    else:
        x = rest[0][...]
    gate_ref, w_ref, b_ref, o_ref = rest[-4:]
    z = _gelu_tanh(_pick(yl_ref, yc_ref, n_lat_tiles))
    u = jnp.dot(z.astype(BF16), w_ref[...], preferred_element_type=F32) + b_ref[...]
    o_ref[...] = x + gate_ref[...] * (z * jax.nn.sigmoid(u))


def _glu_call(lay, y_l, y_c, x, mod, w, bias, layer, j):
    tm, d = TM_TOKEN, lay.d
    split_x = isinstance(x, tuple)
    x_specs = _split_specs(lay, tm) if split_x else [pl.BlockSpec((tm, d), lambda i: (i, 0))]
    return pl.pallas_call(
        functools.partial(_glu_kernel, n_lat_tiles=lay.n_lat // tm, split_x=split_x),
        grid=(lay.n // tm,),
        in_specs=_split_specs(lay, tm) + x_specs + [lay.mod_spec(layer, 2, tm),
                                                    pl.BlockSpec((None, d, d), lambda i: (j, 0, 0)),
                                                    pl.BlockSpec((None, 1, d), lambda i: (j, 0, 0))],
        out_specs=pl.BlockSpec((tm, d), lambda i: (i, 0)),
        out_shape=jax.ShapeDtypeStruct((lay.n, d), F32),
        compiler_params=_params(("parallel",)),
        name="s5_glu",
    )(y_l, y_c, *(x if split_x else (x,)), mod, w, bias)


def _rope(v, cos, sin):
    return v * cos + pltpu.roll(v, LANES // 2, 1) * sin


def _mla_proj_kernel(x_ref, g_ref, sh_ref, sc_ref, cos_ref, sin_ref, wdq_ref, qn_ref, wuq_ref,
                     wdkv_ref, kvn_ref, wukv_ref, q_ref, k_ref, v_ref, *, heads, scale):
    a = _norm_mod(x_ref[...], g_ref[...], sc_ref[...], sh_ref[...]).astype(BF16)
    cos, sin = cos_ref[...], sin_ref[...]
    hd = heads * LANES
    cq = jnp.dot(a, wdq_ref[...], preferred_element_type=F32)
    cq = (_rms(cq) * qn_ref[...]).astype(BF16)
    q = jnp.dot(cq, wuq_ref[...], preferred_element_type=F32)
    ckv = jnp.dot(a, wdkv_ref[...], preferred_element_type=F32)
    kvl = ckv.shape[1] - LANES
    ckvn = (_rms(ckv[:, :kvl]) * kvn_ref[...]).astype(BF16)
    kpe = _rope(ckv[:, kvl:], cos, sin).astype(BF16)
    kv = jnp.dot(ckvn, wukv_ref[...], preferred_element_type=F32)
    for h in range(heads):
        lo, hi_ = h * LANES, (h + 1) * LANES
        q_ref[h, :, 0:LANES] = (q[:, lo:hi_] * scale).astype(BF16)
        q_ref[h, :, LANES:2 * LANES] = (_rope(q[:, hd + lo:hd + hi_], cos, sin) * scale).astype(BF16)
        k_ref[h, :, 0:LANES] = kv[:, lo:hi_].astype(BF16)
        k_ref[h, :, LANES:2 * LANES] = kpe
        v_ref[h] = kv[:, hd + lo:hd + hi_].astype(BF16)


def _mla_proj_call(lay, x, mod, g, cos_t, sin_t, wts, layer, j, heads, scale):
    wdq, qn, wuq, wdkv, kvn, wukv = wts
    tm, d = TM_PROJ, lay.d
    full = lambda arr: pl.BlockSpec((None,) + arr.shape[1:], lambda i: (j, 0, 0),
                                    pipeline_mode=pl.Buffered(1))
    return pl.pallas_call(
        functools.partial(_mla_proj_kernel, heads=heads, scale=scale),
        grid=(lay.n // tm,),
        in_specs=[pl.BlockSpec((tm, d), lambda i: (i, 0)),
                  pl.BlockSpec((None, 1, d), lambda i: (layer, 0, 0)),
                  lay.mod_spec(layer, 0, tm), lay.mod_spec(layer, 1, tm),
                  pl.BlockSpec((tm, LANES), lambda i: (i, 0)),
                  pl.BlockSpec((tm, LANES), lambda i: (i, 0)),
                  full(wdq), full(qn), full(wuq), full(wdkv), full(kvn), full(wukv)],
        out_specs=[pl.BlockSpec((heads, tm, 2 * LANES), lambda i: (0, i, 0)),
                   pl.BlockSpec((heads, tm, 2 * LANES), lambda i: (0, i, 0)),
                   pl.BlockSpec((heads, tm, LANES), lambda i: (0, i, 0))],
        out_shape=[jax.ShapeDtypeStruct((heads, lay.n, 2 * LANES), BF16),
                   jax.ShapeDtypeStruct((heads, lay.n, 2 * LANES), BF16),
                   jax.ShapeDtypeStruct((heads, lay.n, LANES), BF16)],
        compiler_params=_params(("parallel",)),
        name="mla_proj",
    )(x, g, mod, mod, cos_t, sin_t, wdq, qn, wuq, wdkv, kvn, wukv)


def _attn_kernel(*refs, tk, n_kv, n_cast=0):
    q_ref, kc_ref, vc_ref = refs[:3]
    kl_ref, vl_ref = (refs[3], refs[4]) if n_kv else (None, None)
    if n_cast:
        for src, dst in zip(refs[5:5 + n_cast], refs[-n_cast:]):
            dst[...] = src[...].astype(dst.dtype)
    o_ref = refs[-1 - n_cast]
    nt = (((1,), (1,)), ((), ()))
    hps = q_ref.shape[0]
    qs, ms, ls, accs = [], [], [], []
    for h in range(hps):
        q = q_ref[h]
        s = lax.dot_general(q, kc_ref[h], nt, preferred_element_type=F32)
        m = jnp.max(s, axis=-1, keepdims=True)
        p = jnp.exp2(s - m)
        qs.append(q)
        ms.append(m)
        ls.append(jnp.sum(p, axis=-1, keepdims=True))
        accs.append(jnp.dot(p.astype(BF16), vc_ref[h], preferred_element_type=F32))
    for i in range(n_kv):
        for h in range(hps):
            s = lax.dot_general(qs[h], kl_ref[h, i * tk:(i + 1) * tk, :], nt, preferred_element_type=F32)
            m_new = jnp.maximum(ms[h], jnp.max(s, axis=-1, keepdims=True))
            alpha = jnp.exp2(ms[h] - m_new)
            p = jnp.exp2(s - m_new)
            ls[h] = alpha * ls[h] + jnp.sum(p, axis=-1, keepdims=True)
            accs[h] = alpha * accs[h] + jnp.dot(p.astype(BF16), vl_ref[h, i * tk:(i + 1) * tk, :],
                                                preferred_element_type=F32)
            ms[h] = m_new
    for h in range(hps):
        o_ref[:, h * LANES:(h + 1) * LANES] = (accs[h] / ls[h]).astype(o_ref.dtype)


def _attn_call(lay, q, k, v, heads, need_ctx, cast):
    nb, l, lc = lay.b, lay.l, lay.lc
    hps = ATTN_HEADS
    tq = min(TQ_ATTN, l)
    tk = min(TK_ATTN, l)
    nlq = l // tq
    nh = heads // hps
    ctx_blk0 = (nb * l) // lc
    ctx_specs = [pl.BlockSpec((hps, lc, 2 * LANES), lambda b, h, qi: (h, ctx_blk0 + b, 0)),
                 pl.BlockSpec((hps, lc, LANES), lambda b, h, qi: (h, ctx_blk0 + b, 0))]
    arrays, first, count = cast
    steps = nb * nh * nlq
    cast_in, cast_specs, cast_out_specs, cast_shapes = [], [], [], []
    for w in arrays:
        rows, cols = w.shape[1:]
        blk = count * rows // steps
        assert blk * steps == count * rows and blk % CHUNK == 0
        blk0 = first * rows // blk
        cast_in.append(w.reshape(-1, cols))
        cast_specs.append(pl.BlockSpec(
            (blk, cols), functools.partial(lambda b, h, qi, o: (o + (b * nh + h) * nlq + qi, 0), o=blk0)))
        cast_out_specs.append(pl.BlockSpec((blk, cols), lambda b, h, qi: ((b * nh + h) * nlq + qi, 0)))
        cast_shapes.append(jax.ShapeDtypeStruct((count * rows, cols), BF16))
    outs = pl.pallas_call(
        functools.partial(_attn_kernel, tk=tk, n_kv=l // tk, n_cast=len(arrays)),
        grid=(nb, nh, nlq),
        in_specs=[pl.BlockSpec((hps, tq, 2 * LANES), lambda b, h, qi: (h, b * nlq + qi, 0))] + ctx_specs
                 + [pl.BlockSpec((hps, l, 2 * LANES), lambda b, h, qi: (h, b, 0)),
                    pl.BlockSpec((hps, l, LANES), lambda b, h, qi: (h, b, 0))] + cast_specs,
        out_specs=[pl.BlockSpec((tq, hps * LANES), lambda b, h, qi: (b * nlq + qi, h))] + cast_out_specs,
        out_shape=[jax.ShapeDtypeStruct((lay.n_lat, heads * LANES), BF16)] + cast_shapes,
        compiler_params=_params(("parallel", "parallel", "arbitrary")),
        name="mla_attn",
    )(q, k, v, k, v, *cast_in)
    o_lat = outs[0]
    converted = [o.reshape((count,) + w.shape[1:]) for o, w in zip(outs[1:], arrays)]
    if not need_ctx:
        return o_lat, None, converted
    o_ctx = pl.pallas_call(
        functools.partial(_attn_kernel, tk=tk, n_kv=0),
        grid=(nb, heads // hps, 1),
        in_specs=[pl.BlockSpec((hps, lc, 2 * LANES), lambda b, h, qi: (h, ctx_blk0 + b, 0))] + ctx_specs,
        out_specs=pl.BlockSpec((lc, hps * LANES), lambda b, h, qi: (b, h)),
        out_shape=jax.ShapeDtypeStruct((nb * lc, heads * LANES), BF16),
        compiler_params=_params(("parallel", "parallel", "arbitrary")),
        name="mla_attn_ctx",
    )(q, k, v)
    return o_lat, o_ctx, converted


def _oproj_kernel(*refs, n_lat_tiles):
    ol_ref = refs[0]
    oc_ref = refs[1] if len(refs) == 6 else None
    x_ref, gate_ref, w_ref, out_ref = refs[-4:]
    o = ol_ref[...]
    if oc_ref is not None:
        o = jnp.where(pl.program_id(0) < n_lat_tiles, o, oc_ref[...])
    out_ref[...] = x_ref[...] + gate_ref[...] * jnp.dot(o, w_ref[...], preferred_element_type=F32)


def _oproj_call(lay, o_lat, o_ctx, x, mod, w, layer, j, n_rows):
    tm, d = TM_TOKEN, lay.d
    nlt = lay.n_lat // tm
    hd = o_lat.shape[1]
    o_specs = [pl.BlockSpec((tm, hd), lambda i: (jnp.minimum(i, nlt - 1), 0))]
    if o_ctx is not None:
        o_specs.append(pl.BlockSpec((tm, hd), lambda i: (jnp.maximum(i - nlt, 0), 0)))
    return pl.pallas_call(
        functools.partial(_oproj_kernel, n_lat_tiles=nlt),
        grid=(n_rows // tm,),
        in_specs=o_specs + [pl.BlockSpec((tm, d), lambda i: (i, 0)),
                            lay.mod_spec(layer, 2, tm),
                            pl.BlockSpec((None, hd, d), lambda i: (j, 0, 0))],
        out_specs=pl.BlockSpec((tm, d), lambda i: (i, 0)),
        out_shape=jax.ShapeDtypeStruct((n_rows, d), F32),
        compiler_params=_params(("parallel",)),
        name="mla_oproj",
    )(*([o_lat] + ([] if o_ctx is None else [o_ctx]) + [x, mod, w]))


def _to_chunk_major(t):
    b, n, d = t.shape
    return jnp.transpose(t.reshape(b, n // CHUNK, CHUNK, d), (0, 2, 1, 3)).reshape(b * n, d)


def _from_chunk_major(t, b):
    n, d = t.shape[0] // b, t.shape[1]
    return jnp.transpose(t.reshape(b, CHUNK, n // CHUNK, d), (0, 2, 1, 3)).reshape(b, n, d)


def _pad_rope_cols(w):
    nf = w.shape[-1] // 4
    w4 = w.reshape(w.shape[:-1] + (2, 2, nf))
    z = jnp.zeros(w.shape[:-1] + (LANES // 2 - 2 * nf,), w.dtype)
    x1 = w4[..., 0, :].reshape(w.shape[:-1] + (2 * nf,))
    x2 = w4[..., 1, :].reshape(w.shape[:-1] + (2 * nf,))
    return jnp.concatenate([x1, z, x2, z], axis=-1)


def _rope_tables(lay, rope_dim):
    nf = rope_dim // 4
    pos = jnp.arange(lay.l)
    inv = ROPE_THETA ** (-jnp.arange(nf, dtype=F32) / nf)
    row = (pos // GRID_W).astype(F32)
    col = (pos % GRID_W).astype(F32)
    ang = jnp.concatenate([row[:, None] * inv, col[:, None] * inv], axis=1)
    z = jnp.zeros((lay.l, LANES // 2 - 2 * nf), F32)
    cos_l = jnp.concatenate([jnp.cos(ang), z, jnp.cos(ang), z], axis=1)
    sin_l = jnp.concatenate([-jnp.sin(ang), z, jnp.sin(ang), z], axis=1)
    cos_l = _to_chunk_major(jnp.broadcast_to(cos_l[None], (lay.b, lay.l, LANES)))
    sin_l = _to_chunk_major(jnp.broadcast_to(sin_l[None], (lay.b, lay.l, LANES)))
    n_ctx = lay.b * lay.lc
    cos_t = jnp.concatenate([cos_l, jnp.ones((n_ctx, LANES), F32)], axis=0)
    sin_t = jnp.concatenate([sin_l, jnp.zeros((n_ctx, LANES), F32)], axis=0)
    return cos_t, sin_t


def _mla_weights(w_dq, q_norm, w_uq, w_dkv, kv_norm, w_ukv, heads, nope, rope, vdim):
    n, ql = w_uq.shape[:2]
    kvl = kv_norm.shape[1]
    uq = w_uq.reshape(n, ql, heads, nope + rope)
    uq_pe = _pad_rope_cols(uq[..., nope:]).reshape(n, ql, heads * LANES)
    wuq = jnp.concatenate([uq[..., :nope].reshape(n, ql, heads * nope), uq_pe], axis=-1)
    wdkv = jnp.concatenate([w_dkv[..., :kvl], _pad_rope_cols(w_dkv[..., kvl:])], axis=-1)
    ukv = w_ukv.reshape(n, kvl, heads, nope + vdim)
    wukv = jnp.concatenate([ukv[..., :nope].reshape(n, kvl, heads * nope),
                            ukv[..., nope:].reshape(n, kvl, heads * vdim)], axis=-1)
    return (w_dq.astype(BF16), q_norm.reshape(n, 1, ql).astype(F32), wuq.astype(BF16),
            wdkv.astype(BF16), kv_norm.reshape(n, 1, kvl).astype(F32), wukv.astype(BF16))


def kernel(x, c, ctx, c_ctx, ada_w, ada_b, norm1_g, norm2_g, final_norm_g, s5_lam_re, s5_lam_im, s5_log_dt, s5_b_re, s5_b_im, s5_c_re, s5_c_im, s5_d, s5_w_glu, s5_b_glu, mla_w_dq, mla_q_norm, mla_w_uq, mla_w_dkv, mla_kv_norm, mla_w_ukv, mla_w_o, ffn_w1, ffn_w2):
    b, l, d = x.shape
    lc = ctx.shape[1]
    depth = ada_w.shape[0]
    lay = _Layout(b, l, lc, d)

    kvl = mla_kv_norm.shape[1]
    rope = mla_w_dkv.shape[2] - kvl
    heads = (mla_w_uq.shape[2] + mla_w_o.shape[1] - mla_w_ukv.shape[2]) // rope
    vdim = mla_w_o.shape[1] // heads
    nope = mla_w_uq.shape[2] // heads - rope
    assert nope == LANES and vdim == LANES and 4 * (rope // 4) == rope and rope <= LANES
    assert b + 1 <= SUBLANES and s5_b_re.shape[-1] == CHUNK
    scale = (nope + rope) ** -0.5 * math.log2(math.e)

    xs = (_to_chunk_major(x), _to_chunk_major(ctx))

    cc = jnp.concatenate([c, c_ctx[None]], axis=0)
    mod = _ada_table(cc, ada_w, ada_b).reshape(depth, SUBLANES, 1, 6 * d)
    g1 = norm1_g.reshape(depth, 1, d)
    g2 = norm2_g.reshape(depth, 1, d)
    cos_t, sin_t = _rope_tables(lay, rope)

    tabs = _s5_tables(s5_lam_re, s5_lam_im, s5_log_dt, s5_b_re, s5_b_im, s5_c_re, s5_c_im, s5_d)
    wts = _mla_weights(mla_w_dq, mla_q_norm, mla_w_uq, mla_w_dkv, mla_kv_norm, mla_w_ukv,
                       heads, nope, rope, vdim)
    w_glu, b_glu = s5_w_glu.astype(BF16), s5_b_glu.reshape(-1, 1, d)
    w_o = mla_w_o.astype(BF16)
    mlp_w = {0: (_cast_layer_call(ffn_w1, 0), _cast_layer_call(ffn_w2, 0), 0)}

    a_next = None
    for i in range(depth):
        j = i // 2
        n_rows = lay.n_lat if i == depth - 1 else lay.n
        if i % 2 == 0:
            if a_next is None:
                assert i == 0
                a_next = _norm_call(lay, xs[0], xs[1], mod, g1, i)
            y_l, y_c = _s5_core_call(lay, a_next, tabs, j)
            xs = _glu_call(lay, y_l, y_c, xs, mod, w_glu, b_glu, i, j)
        else:
            q, k, v = _mla_proj_call(lay, xs, mod, g1, cos_t, sin_t, wts, i, j, heads, scale)
            count = min(2, depth - i)
            o_lat, o_ctx, (w1c, w2c) = _attn_call(lay, q, k, v, heads, n_rows == lay.n,
                                                  ([ffn_w1, ffn_w2], i, count))
            for t in range(count):
                mlp_w[i + t] = (w1c, w2c, t)
            xs = _oproj_call(lay, o_lat, o_ctx, xs, mod, w_o, i, j, n_rows)
        w1, w2, wl = mlp_w[i]
        final_g = final_norm_g.reshape(1, d) if i == depth - 1 else None
        feeds_s5 = i + 1 < depth and (i + 1) % 2 == 0
        res = _ffn_call(lay, xs, mod, g2, w1, w2, i, wl, n_rows, final_g, g1 if feeds_s5 else None)
        xs, a_next = res if feeds_s5 else (res, None)

    return _from_chunk_major(xs, b)
```
